```python
import math
import jax, jax.numpy as jnp
from jax import lax
import numpy as np

D_MODEL = 2048
BATCH = 4
SEQ = 4096
DEPTH = 2

D_FF = 5632
SGU_WIDTH = D_MODEL // 2
SGU_GROUPS = 8
SGU_GROUP_CH = SGU_WIDTH // SGU_GROUPS
SGU_CHUNK = 128
ATT_HEADS = 8
ATT_HEAD_DIM = 128
ATT_WIDTH = ATT_HEADS * ATT_HEAD_DIM
IDX_HEADS = 16
IDX_DIM = 64
TOPK_MAX = 256
Q_BLOCK = 128
NUM_BUCKETS = 32
MAX_DISTANCE = 128
NORM_EPS = 1e-6
LN_EPS = 1e-5

IN_SIZES = (SGU_WIDTH, SGU_WIDTH, ATT_WIDTH, ATT_WIDTH, ATT_WIDTH,
            IDX_HEADS * IDX_DIM, IDX_DIM, IDX_HEADS)
IN_COLS = sum(IN_SIZES)
IN_SPLITS = tuple(int(s) for s in np.cumsum(IN_SIZES)[:-1])

kernel_name = "hybrid_gated_sgu_dsa_macaron"


def rmsnorm(x, g):
    xf = x.astype(jnp.float32)
    y = xf * lax.rsqrt(jnp.mean(xf * xf, axis=-1, keepdims=True) + NORM_EPS)
    return (y * g.astype(jnp.float32)).astype(x.dtype)


def layernorm(x, g, b):
    xf = x.astype(jnp.float32)
    mu = jnp.mean(xf, axis=-1, keepdims=True)
    var = jnp.mean(jnp.square(xf - mu), axis=-1, keepdims=True)
    y = (xf - mu) * lax.rsqrt(var + LN_EPS)
    return (y * g.astype(jnp.float32) + b.astype(jnp.float32)).astype(x.dtype)


def swiglu(h, w_in, w_out):
    a, b = jnp.split(h @ w_in, 2, axis=-1)
    return (jax.nn.silu(a) * b) @ w_out


def t5_causal_bucket(n):
    max_exact = NUM_BUCKETS // 2
    nf = jnp.maximum(n, 1).astype(jnp.float32)
    large = max_exact + (jnp.log(nf / max_exact) / math.log(MAX_DISTANCE / max_exact)
                         * (NUM_BUCKETS - max_exact)).astype(jnp.int32)
    large = jnp.minimum(large, NUM_BUCKETS - 1)
    return jnp.where(n < max_exact, n, large)


def spatial_gating(z_u, z_v, ln_g, ln_b, w_s, b_s):
    B, S, _ = z_v.shape
    n_chunks = S // SGU_CHUNK
    v = layernorm(z_v, ln_g, ln_b).reshape(B, n_chunks, SGU_CHUNK, SGU_GROUPS, SGU_GROUP_CH)
    causal = jnp.tril(jnp.ones((SGU_CHUNK, SGU_CHUNK), dtype=bool))
    ws = jnp.where(causal[None], w_s, jnp.zeros_like(w_s))
    mixed = jnp.einsum('gts,bnsgc->bntgc', ws, v) + b_s.T[None, None, :, :, None]
    return z_u * mixed.reshape(B, S, SGU_WIDTH)


def dsa_attention(q, k, v, q_idx, k_idx, w_idx, rel_bias):
    B, S = q.shape[0], q.shape[1]
    top_k = min(TOPK_MAX, S // 4)
    n_blocks = S // Q_BLOCK
    key_pos = jnp.arange(S, dtype=jnp.int32)
    idx_scale = IDX_DIM ** -0.5
    head_w_scale = IDX_HEADS ** -0.5
    att_scale = ATT_HEAD_DIM ** -0.5

    def to_blocks(a):
        return jnp.moveaxis(a.reshape(B, n_blocks, Q_BLOCK, *a.shape[2:]), 1, 0)

    gather = jax.vmap(lambda table, ids: table[ids])

    def block(args):
        qb, qib, wb, start = args
        q_pos = start + jnp.arange(Q_BLOCK, dtype=jnp.int32)
        causal = key_pos[None, :] <= q_pos[:, None]
        dots = jnp.einsum('bthd,bsd->bths', qib, k_idx).astype(jnp.float32) * idx_scale
        score = jnp.einsum('bths,bth->bts', jax.nn.relu(dots), wb.astype(jnp.float32) * head_w_scale)
        score = jnp.where(causal[None], score, -jnp.inf)
        _, sel = lax.top_k(score, top_k)
        k_sel = gather(k, sel)
        v_sel = gather(v, sel)
        dist = q_pos[None, :, None] - sel
        valid = dist >= 0
        bias = rel_bias[t5_causal_bucket(jnp.maximum(dist, 0))]
        logits = jnp.einsum('bthd,btkhd->bthk', qb, k_sel).astype(jnp.float32) * att_scale
        logits = logits + jnp.moveaxis(bias.astype(jnp.float32), -1, 2)
        logits = jnp.where(valid[:, :, None, :], logits, -jnp.inf)
        p = jax.nn.softmax(logits, axis=-1).astype(v.dtype)
        return jnp.einsum('bthk,btkhd->bthd', p, v_sel)

    starts = jnp.arange(n_blocks, dtype=jnp.int32) * Q_BLOCK
    out = lax.map(block, (to_blocks(q), to_blocks(q_idx), to_blocks(w_idx), starts))
    return jnp.moveaxis(out, 0, 1).reshape(B, S, ATT_WIDTH)


def setup_inputs(seed: int = 0) -> dict:
    key = jax.random.key(seed)
    ks = jax.random.split(key, 24)
    f32 = jnp.float32

    def w(k, shape, fan_in):
        return jax.random.normal(k, shape, f32) * fan_in ** -0.5

    def gain(k, shape):
        return 1.0 + 0.02 * jax.random.normal(k, shape, f32)

    L = DEPTH
    return {
        "x": jax.random.normal(ks[0], (BATCH, SEQ, D_MODEL), f32),
        "ffn1_norm_pre": gain(ks[1], (L, D_MODEL)),
        "ffn1_norm_post": gain(ks[2], (L, D_MODEL)),
        "ffn1_w_in": w(ks[3], (L, D_MODEL, 2 * D_FF), D_MODEL),
        "ffn1_w_out": w(ks[4], (L, D_FF, D_MODEL), D_FF),
        "mix_norm_pre": gain(ks[5], (L, D_MODEL)),
        "mix_norm_post": gain(ks[6], (L, D_MODEL)),
        "w_in": w(ks[7], (L, D_MODEL, IN_COLS), D_MODEL),
        "sgu_ln_g": gain(ks[8], (L, SGU_WIDTH)),
        "sgu_ln_b": 0.02 * jax.random.normal(ks[9], (L, SGU_WIDTH), f32),
        "sgu_w_s": w(ks[10], (L, SGU_GROUPS, SGU_CHUNK, SGU_CHUNK), SGU_CHUNK),
        "sgu_b": gain(ks[11], (L, SGU_GROUPS, SGU_CHUNK)),
        "rel_bias": 0.5 * jax.random.normal(ks[12], (NUM_BUCKETS, ATT_HEADS), f32),
        "w_branch_a": w(ks[13], (L, SGU_WIDTH, D_MODEL), SGU_WIDTH),
        "w_branch_b": w(ks[14], (L, ATT_WIDTH, D_MODEL), ATT_WIDTH),
        "w_gate": w(ks[15], (L, D_MODEL, 2 * D_MODEL), D_MODEL),
        "w_out": w(ks[16], (L, D_MODEL, D_MODEL), D_MODEL),
        "ffn2_norm_pre": gain(ks[17], (L, D_MODEL)),
        "ffn2_norm_post": gain(ks[18], (L, D_MODEL)),
        "ffn2_w_in": w(ks[19], (L, D_MODEL, 2 * D_FF), D_MODEL),
        "ffn2_w_out": w(ks[20], (L, D_FF, D_MODEL), D_FF),
    }


def reference(x, ffn1_norm_pre, ffn1_norm_post, ffn1_w_in, ffn1_w_out,
              mix_norm_pre, mix_norm_post, w_in, sgu_ln_g, sgu_ln_b, sgu_w_s, sgu_b,
              rel_bias, w_branch_a, w_branch_b, w_gate, w_out,
              ffn2_norm_pre, ffn2_norm_post, ffn2_w_in, ffn2_w_out):
    B, S, _ = x.shape
    for l in range(DEPTH):
        f = swiglu(rmsnorm(x, ffn1_norm_pre[l]), ffn1_w_in[l], ffn1_w_out[l])
        x = x + 0.5 * rmsnorm(f, ffn1_norm_post[l])

        h = rmsnorm(x, mix_norm_pre[l])
        z_u, z_v, q, k, v, q_idx, k_idx, w_idx = jnp.split(h @ w_in[l], IN_SPLITS, axis=-1)

        y_a = spatial_gating(jax.nn.gelu(z_u), jax.nn.gelu(z_v),
                             sgu_ln_g[l], sgu_ln_b[l], sgu_w_s[l], sgu_b[l])

        y_b = dsa_attention(q.reshape(B, S, ATT_HEADS, ATT_HEAD_DIM),
                            k.reshape(B, S, ATT_HEADS, ATT_HEAD_DIM),
                            v.reshape(B, S, ATT_HEADS, ATT_HEAD_DIM),
                            q_idx.reshape(B, S, IDX_HEADS, IDX_DIM),
                            k_idx, w_idx, rel_bias)

        g_a, g_b = jnp.split(jax.nn.sigmoid(h @ w_gate[l]), 2, axis=-1)
        merged = g_a * (y_a @ w_branch_a[l]) + g_b * (y_b @ w_branch_b[l])
        x = x + rmsnorm(merged @ w_out[l], mix_norm_post[l])

        f = swiglu(rmsnorm(x, ffn2_norm_pre[l]), ffn2_w_in[l], ffn2_w_out[l])
        x = x + 0.5 * rmsnorm(f, ffn2_norm_post[l])
    return x
```

```python
import functools
import math

import jax
import jax.numpy as jnp
from jax import lax
from jax.experimental import pallas as pl
from jax.experimental.pallas import tpu as pltpu

F32 = jnp.float32
BF16 = jnp.bfloat16
I32 = jnp.int32

SGU_GROUPS = 8
SGU_CHUNK = 128
ATT_HEADS = 8
ATT_HEAD_DIM = 128
IDX_HEADS = 16
IDX_DIM = 64
TOPK_MAX = 256
NUM_BUCKETS = 32
MAX_DISTANCE = 128
NORM_EPS = 1e-6
LN_EPS = 1e-5

V7X_VMEM_BYTES = 64 * 1024 * 1024
VMEM_LIMIT = V7X_VMEM_BYTES - 8 * 1024 * 1024

FFN_TM = 512
FFN_TF = 512
PROJ_TM = 512
PROJ_BN = 1024
MERGE_TM = 256
ATT_TQ = 256
ATT_KC = 128

INT_MIN = -(2 ** 31)
MASKED_LOGIT = -1e30


def _rms(xf, g):
    ms = jnp.mean(xf * xf, axis=-1, keepdims=True)
    return xf * lax.rsqrt(ms + NORM_EPS) * g


def _gelu_tanh(x):
    c = math.sqrt(2.0 / math.pi)
    return x * (0.5 * (1.0 + jnp.tanh(c * (x + 0.044715 * (x * x * x)))))


def _dot(a, b):
    return jnp.dot(a, b, preferred_element_type=F32)


def _dot_nt(a, b):
    return lax.dot_general(a, b, (((1,), (1,)), ((), ())), preferred_element_type=F32)


def _ffn_body(x_ref, gpre_ref, gpost_ref, wa_ref, wb_ref, wo_ref, o_ref, h_scr, acc_scr):
    j = pl.program_id(1)

    @pl.when(j == 0)
    def _():
        h_scr[...] = _rms(x_ref[...], gpre_ref[...]).astype(BF16)
        acc_scr[...] = jnp.zeros_like(acc_scr)

    h = h_scr[...]
    a = _dot(h, wa_ref[...])
    b = _dot(h, wb_ref[...])
    g = (a * jax.nn.sigmoid(a) * b).astype(BF16)
    acc_scr[...] += _dot(g, wo_ref[...])

    @pl.when(j == pl.num_programs(1) - 1)
    def _():
        o_ref[...] = x_ref[...] + 0.5 * _rms(acc_scr[...], gpost_ref[...])


def _ffn(x2, g_pre, g_post, w_in, w_out):
    n, d = x2.shape
    d_ff = w_out.shape[0]
    nf = d_ff // FFN_TF
    return pl.pallas_call(
        _ffn_body,
        grid=(n // FFN_TM, nf),
        in_specs=[
            pl.BlockSpec((FFN_TM, d), lambda i, j: (i, 0)),
            pl.BlockSpec((1, d), lambda i, j: (0, 0)),
            pl.BlockSpec((1, d), lambda i, j: (0, 0)),
            pl.BlockSpec((d, FFN_TF), lambda i, j: (0, j)),
            pl.BlockSpec((d, FFN_TF), lambda i, j: (0, j + nf)),
            pl.BlockSpec((FFN_TF, d), lambda i, j: (j, 0)),
        ],
        out_specs=pl.BlockSpec((FFN_TM, d), lambda i, j: (i, 0)),
        out_shape=jax.ShapeDtypeStruct((n, d), F32),
        scratch_shapes=[pltpu.VMEM((FFN_TM, d), BF16), pltpu.VMEM((FFN_TM, d), F32)],
        compiler_params=pltpu.CompilerParams(
            dimension_semantics=("parallel", "arbitrary"), vmem_limit_bytes=VMEM_LIMIT),
        name="ffn",
    )(x2, g_pre.reshape(1, d), g_post.reshape(1, d), w_in, w_in, w_out)


def _proj_nat_body(x_ref, g_ref, w_ref, wki_ref, lng_ref, lnb_ref,
                   u_ref, vln_ref, k_ref, gate_ref, kidx_ref, h_scr):
    j = pl.program_id(1)

    @pl.when(j == 0)
    def _():
        h = _rms(x_ref[...], g_ref[...]).astype(BF16)
        h_scr[...] = h
        kidx_ref[...] = _dot(h, wki_ref[...]).astype(BF16)

    z = _dot(h_scr[...], w_ref[...])

    @pl.when(j == 0)
    def _():
        u_ref[...] = _gelu_tanh(z).astype(BF16)

    @pl.when(j == 1)
    def _():
        v = _gelu_tanh(z)
        mu = jnp.mean(v, axis=-1, keepdims=True)
        vc = v - mu
        var = jnp.mean(vc * vc, axis=-1, keepdims=True)
        vln_ref[...] = (vc * lax.rsqrt(var + LN_EPS) * lng_ref[...] + lnb_ref[...]).astype(BF16)

    @pl.when(j == 2)
    def _():
        k_ref[...] = z.astype(BF16)

    @pl.when(j >= 3)
    def _():
        gate_ref[...] = jax.nn.sigmoid(z).astype(BF16)


def _proj_nat(x2, g, w_cat, w_kidx, ln_g, ln_b):
    n, d = x2.shape
    ncol = w_cat.shape[1]
    width = PROJ_BN
    nj = ncol // width
    ngate = ncol - 3 * width
    tok = lambda i, j: (i, 0)
    return pl.pallas_call(
        _proj_nat_body,
        grid=(n // PROJ_TM, nj),
        in_specs=[
            pl.BlockSpec((PROJ_TM, d), tok),
            pl.BlockSpec((1, d), lambda i, j: (0, 0)),
            pl.BlockSpec((d, width), lambda i, j: (0, j)),
            pl.BlockSpec((d, IDX_DIM), lambda i, j: (0, 0)),
            pl.BlockSpec((1, width), lambda i, j: (0, 0)),
            pl.BlockSpec((1, width), lambda i, j: (0, 0)),
        ],
        out_specs=[
            pl.BlockSpec((PROJ_TM, width), tok),
            pl.BlockSpec((PROJ_TM, width), tok),
            pl.BlockSpec((PROJ_TM, width), tok),
            pl.BlockSpec((PROJ_TM, width), lambda i, j: (i, jnp.maximum(j - 3, 0))),
            pl.BlockSpec((PROJ_TM, IDX_DIM), tok),
        ],
        out_shape=[
            jax.ShapeDtypeStruct((n, width), BF16),
            jax.ShapeDtypeStruct((n, width), BF16),
            jax.ShapeDtypeStruct((n, width), BF16),
            jax.ShapeDtypeStruct((n, ngate), BF16),
            jax.ShapeDtypeStruct((n, IDX_DIM), BF16),
        ],
        scratch_shapes=[pltpu.VMEM((PROJ_TM, d), BF16)],
        compiler_params=pltpu.CompilerParams(
            dimension_semantics=("parallel", "arbitrary"), vmem_limit_bytes=VMEM_LIMIT),
        name="proj_nat",
    )(x2, g.reshape(1, d), w_cat, w_kidx, ln_g.reshape(1, width), ln_b.reshape(1, width))


def _proj_t_body(x_ref, g_ref, wt_ref, wwt_ref, qt_ref, vt_ref, qit_ref, wit_ref, h_scr):
    j = pl.program_id(2)

    @pl.when(j == 0)
    def _():
        h = _rms(x_ref[0], g_ref[...]).astype(BF16)
        h_scr[...] = h
        wit_ref[0] = _dot_nt(wwt_ref[...], h)

    zt = _dot_nt(wt_ref[...], h_scr[...]).astype(BF16)

    @pl.when(j == 0)
    def _():
        qt_ref[0] = zt

    @pl.when(j == 1)
    def _():
        for c in range(PROJ_TM // ATT_KC):
            vt_ref[0, c] = zt[:, c * ATT_KC:(c + 1) * ATT_KC]

    @pl.when(j == 2)
    def _():
        qit_ref[0] = zt


def _proj_t(x3, g, wt_cat, wwt):
    b, s, d = x3.shape
    width = PROJ_BN
    nch = PROJ_TM // ATT_KC
    feat = lambda bi, si, j: (bi, 0, si)
    return pl.pallas_call(
        _proj_t_body,
        grid=(b, s // PROJ_TM, 3),
        in_specs=[
            pl.BlockSpec((1, PROJ_TM, d), lambda bi, si, j: (bi, si, 0)),
            pl.BlockSpec((1, d), lambda bi, si, j: (0, 0)),
            pl.BlockSpec((width, d), lambda bi, si, j: (j, 0)),
            pl.BlockSpec((IDX_HEADS, d), lambda bi, si, j: (0, 0)),
        ],
        out_specs=[
            pl.BlockSpec((1, width, PROJ_TM), feat),
            pl.BlockSpec((1, nch, width, ATT_KC), lambda bi, si, j: (bi, si, 0, 0)),
            pl.BlockSpec((1, width, PROJ_TM), feat),
            pl.BlockSpec((1, IDX_HEADS, PROJ_TM), feat),
        ],
        out_shape=[
            jax.ShapeDtypeStruct((b, width, s), BF16),
            jax.ShapeDtypeStruct((b, s // ATT_KC, width, ATT_KC), BF16),
            jax.ShapeDtypeStruct((b, width, s), BF16),
            jax.ShapeDtypeStruct((b, IDX_HEADS, s), F32),
        ],
        scratch_shapes=[pltpu.VMEM((PROJ_TM, d), BF16)],
        compiler_params=pltpu.CompilerParams(
            dimension_semantics=("parallel", "parallel", "arbitrary"), vmem_limit_bytes=VMEM_LIMIT),
        name="proj_t",
    )(x3, g.reshape(1, d), wt_cat, wwt)


N_BIAS_TILES = 4


def _bias_body(rb_ref, o_ref):
    h = pl.program_id(0)
    r = lax.broadcasted_iota(I32, (ATT_KC, ATT_TQ), 0)
    c = lax.broadcasted_iota(I32, (ATT_KC, ATT_TQ), 1)
    max_exact = NUM_BUCKETS // 2
    for t in range(N_BIAS_TILES):
        dist = jnp.maximum((t - 1) * ATT_KC + c - r, 0)
        nf = jnp.maximum(dist, 1).astype(F32)
        large = max_exact + (jnp.log(nf / max_exact) / math.log(MAX_DISTANCE / max_exact)
                             * (NUM_BUCKETS - max_exact)).astype(I32)
        large = jnp.minimum(large, NUM_BUCKETS - 1)
        bucket = jnp.where(dist < max_exact, dist, large)
        val = jnp.zeros((ATT_KC, ATT_TQ), F32)
        for bkt in range(NUM_BUCKETS):
            val = jnp.where(bucket == bkt, rb_ref[bkt, h], val)
        o_ref[0, t] = val


def _bias_tiles(rel_bias):
    assert (N_BIAS_TILES - 2) * ATT_KC - (ATT_KC - 1) >= MAX_DISTANCE
    return pl.pallas_call(
        _bias_body,
        grid=(ATT_HEADS,),
        in_specs=[pl.BlockSpec(memory_space=pltpu.SMEM)],
        out_specs=pl.BlockSpec((1, N_BIAS_TILES, ATT_KC, ATT_TQ), lambda h: (h, 0, 0, 0)),
        out_shape=jax.ShapeDtypeStruct((ATT_HEADS, N_BIAS_TILES, ATT_KC, ATT_TQ), F32),
        name="bias_tiles",
    )(rel_bias)


def _attn_body(kidx_ref, qit_ref, wit_ref, k_ref, vt_ref, qt_ref, bias_ref, o_ref,
               key_scr, s_scr, *, top_k):
    i = pl.program_id(1)
    kc, tq = ATT_KC, ATT_TQ
    nch = (i + 1) * (tq // kc)
    q0 = i * tq

    def rows8(x):
        return x.reshape(kc // 8, 8, tq)

    w_all = wit_ref[0] * (IDX_HEADS ** -0.5 * IDX_DIM ** -0.5)
    row = lax.broadcasted_iota(I32, (kc, tq), 0)
    col = lax.broadcasted_iota(I32, (kc, tq), 1)

    def score_chunk(c, carry):
        r0 = pl.multiple_of(c * kc, kc)
        kch = kidx_ref[0, pl.ds(r0, kc), :]
        acc = jnp.zeros((kc, tq), F32)
        for h in range(IDX_HEADS):
            d = _dot(kch, qit_ref[0, h * IDX_DIM:(h + 1) * IDX_DIM, :])
            acc = acc + jnp.maximum(d, 0.0) * w_all[h:h + 1, :]
        bits = lax.bitcast_convert_type(acc, I32)
        key = jnp.where(bits < 0, bits ^ jnp.int32(0x7FFFFFFF), bits)
        key = jnp.where(r0 + row <= q0 + col, key, INT_MIN)
        key_scr[pl.ds(r0, kc), :] = key
        return carry

    lax.fori_loop(0, nch, score_chunk, 0)

    def count_ge(cand):
        def body(c, cnt):
            r0 = pl.multiple_of(c * kc, kc)
            blk = key_scr[pl.ds(r0, kc), :]
            return cnt + jnp.sum(rows8(jnp.where(blk >= cand, 1, 0)), axis=0)
        cnt8 = lax.fori_loop(0, nch, body, jnp.zeros((8, tq), I32))
        return jnp.sum(cnt8, axis=0, keepdims=True)

    zero = jnp.zeros((1, tq), I32)
    prefix = jnp.where(count_ge(zero) >= top_k, zero, INT_MIN)

    def descend(t, prefix):
        cand = prefix | jnp.left_shift(jnp.int32(1), 30 - t)
        return jnp.where(count_ge(cand) >= top_k, cand, prefix)

    prefix = lax.fori_loop(0, 31, descend, prefix)
    thr = jnp.maximum(prefix, INT_MIN + 1)

    att_scale = ATT_HEAD_DIM ** -0.5
    for h in range(ATT_HEADS):
        hs = slice(h * ATT_HEAD_DIM, (h + 1) * ATT_HEAD_DIM)
        qh = qt_ref[0, hs, :]

        def logits_chunk(c, m8):
            r0 = pl.multiple_of(c * kc, kc)
            s = _dot(k_ref[0, pl.ds(r0, kc), hs], qh) * att_scale
            tile = jnp.minimum((tq // kc) * i - c + 1, N_BIAS_TILES - 1)
            s = s + bias_ref[h, tile]
            s = jnp.where(key_scr[pl.ds(r0, kc), :] >= thr, s, MASKED_LOGIT)
            s_scr[pl.ds(r0, kc), :] = s
            return jnp.maximum(m8, jnp.max(rows8(s), axis=0))

        m8 = lax.fori_loop(0, nch, logits_chunk, jnp.full((8, tq), MASKED_LOGIT, F32))
        m = jnp.max(m8, axis=0, keepdims=True)

        def pv_chunk(c, carry):
            l8, acc = carry
            r0 = pl.multiple_of(c * kc, kc)
            p = jnp.exp(s_scr[pl.ds(r0, kc), :] - m)
            l8 = l8 + jnp.sum(rows8(p), axis=0)
            acc = acc + _dot(vt_ref[0, c, hs, :], p.astype(BF16))
            return l8, acc

        l8, acc = lax.fori_loop(
            0, nch, pv_chunk,
            (jnp.zeros((8, tq), F32), jnp.zeros((ATT_HEAD_DIM, tq), F32)))
        l = jnp.sum(l8, axis=0, keepdims=True)
        o_ref[0, :, hs] = (acc / l).T.astype(BF16)


def _attention(kidx, qit, wit, k, vt, qt, bias, top_k):
    b, s, width = k.shape
    one = pl.Buffered(1)
    return pl.pallas_call(
        functools.partial(_attn_body, top_k=top_k),
        grid=(b, s // ATT_TQ),
        in_specs=[
            pl.BlockSpec((1, s, IDX_DIM), lambda bi, i: (bi, 0, 0), pipeline_mode=one),
            pl.BlockSpec((1, IDX_HEADS * IDX_DIM, ATT_TQ), lambda bi, i: (bi, 0, i)),
            pl.BlockSpec((1, IDX_HEADS, ATT_TQ), lambda bi, i: (bi, 0, i)),
            pl.BlockSpec((1, s, width), lambda bi, i: (bi, 0, 0), pipeline_mode=one),
            pl.BlockSpec((1, s // ATT_KC, width, ATT_KC), lambda bi, i: (bi, 0, 0, 0), pipeline_mode=one),
            pl.BlockSpec((1, width, ATT_TQ), lambda bi, i: (bi, 0, i)),
            pl.BlockSpec(bias.shape, lambda bi, i: (0, 0, 0, 0), pipeline_mode=one),
        ],
        out_specs=pl.BlockSpec((1, ATT_TQ, width), lambda bi, i: (bi, i, 0)),
        out_shape=jax.ShapeDtypeStruct((b, s, width), BF16),
        scratch_shapes=[pltpu.VMEM((s, ATT_TQ), I32), pltpu.VMEM((s, ATT_TQ), F32)],
        compiler_params=pltpu.CompilerParams(
            dimension_semantics=("parallel", "arbitrary"), vmem_limit_bytes=VMEM_LIMIT),
        name="dsa_attention",
    )(kidx, qit, wit, k, vt, qt, bias)


def _merge_body(x_ref, u_ref, vln_ref, yb_ref, gate_ref, ws_ref, bs_ref, wa_ref, wb_ref, wo_ref,
                gpost_ref, o_ref, ya_scr):
    ch = SGU_CHUNK
    d = x_ref.shape[1]
    tril = (lax.broadcasted_iota(I32, (ch, ch), 0) >= lax.broadcasted_iota(I32, (ch, ch), 1))
    for g in range(SGU_GROUPS):
        gs = slice(g * ch, (g + 1) * ch)
        wsg = jnp.where(tril, ws_ref[g], 0.0).astype(BF16)
        for c in range(MERGE_TM // ch):
            cs = slice(c * ch, (c + 1) * ch)
            mixed = _dot(wsg, vln_ref[cs, gs]) + bs_ref[g]
            ya_scr[cs, gs] = (u_ref[cs, gs].astype(F32) * mixed).astype(BF16)
    ma = _dot(ya_scr[...], wa_ref[...])
    mb = _dot(yb_ref[...], wb_ref[...])
    merged = gate_ref[:, :d].astype(F32) * ma + gate_ref[:, d:].astype(F32) * mb
    o = _dot(merged.astype(BF16), wo_ref[...])
    o_ref[...] = x_ref[...] + _rms(o, gpost_ref[...])


def _merge(x2, u, vln, yb, gate, w_s, b_s, w_a, w_b, w_o, g_post):
    n, d = x2.shape
    width = u.shape[1]
    one = pl.Buffered(1)
    tok = lambda i: (i, 0)
    const2 = lambda i: (0, 0)
    const3 = lambda i: (0, 0, 0)
    return pl.pallas_call(
        _merge_body,
        grid=(n // MERGE_TM,),
        in_specs=[
            pl.BlockSpec((MERGE_TM, d), tok),
            pl.BlockSpec((MERGE_TM, width), tok),
            pl.BlockSpec((MERGE_TM, width), tok),
            pl.BlockSpec((MERGE_TM, width), tok),
            pl.BlockSpec((MERGE_TM, 2 * d), tok),
            pl.BlockSpec(w_s.shape, const3, pipeline_mode=one),
            pl.BlockSpec(b_s.shape, const3, pipeline_mode=one),
            pl.BlockSpec(w_a.shape, const2, pipeline_mode=one),
            pl.BlockSpec(w_b.shape, const2, pipeline_mode=one),
            pl.BlockSpec(w_o.shape, const2, pipeline_mode=one),
            pl.BlockSpec((1, d), const2),
        ],
        out_specs=pl.BlockSpec((MERGE_TM, d), tok),
        out_shape=jax.ShapeDtypeStruct((n, d), F32),
        scratch_shapes=[pltpu.VMEM((MERGE_TM, width), BF16)],
        compiler_params=pltpu.CompilerParams(
            dimension_semantics=("parallel",), vmem_limit_bytes=VMEM_LIMIT),
        name="merge",
    )(x2, u, vln, yb, gate, w_s, b_s, w_a, w_b, w_o, g_post.reshape(1, d))


def kernel(x, ffn1_norm_pre, ffn1_norm_post, ffn1_w_in, ffn1_w_out, mix_norm_pre, mix_norm_post, w_in,
           sgu_ln_g, sgu_ln_b, sgu_w_s, sgu_b, rel_bias, w_branch_a, w_branch_b, w_gate, w_out,
           ffn2_norm_pre, ffn2_norm_post, ffn2_w_in, ffn2_w_out):
    b, s, d = x.shape
    depth = w_in.shape[0]
    sgu_w = sgu_ln_g.shape[1]
    att_w = ATT_HEADS * ATT_HEAD_DIM
    idx_w = IDX_HEADS * IDX_DIM
    top_k = min(TOPK_MAX, s // 4)
    assert s % PROJ_TM == 0 and s % ATT_TQ == 0 and (b * s) % FFN_TM == 0
    assert sgu_w == PROJ_BN and att_w == PROJ_BN and idx_w == PROJ_BN

    sizes = (sgu_w, sgu_w, att_w, att_w, att_w, idx_w, IDX_DIM, IDX_HEADS)
    offs = [0]
    for sz in sizes:
        offs.append(offs[-1] + sz)
    col = lambda w, idx: w[:, offs[idx]:offs[idx + 1]]

    bias = _bias_tiles(rel_bias)
    x2 = x.reshape(b * s, d)
    for l in range(depth):
        wl = w_in[l]
        w_nat = jnp.concatenate([col(wl, 0), col(wl, 1), col(wl, 3), w_gate[l]], axis=1).astype(BF16)
        w_kidx = col(wl, 6).astype(BF16)
        wt_cat = jnp.concatenate([col(wl, 2), col(wl, 4), col(wl, 5)], axis=1).T.astype(BF16)
        wwt = col(wl, 7).T.astype(BF16)

        x2 = _ffn(x2, ffn1_norm_pre[l], ffn1_norm_post[l],
                  ffn1_w_in[l].astype(BF16), ffn1_w_out[l].astype(BF16))

        u, vln, k, gate, kidx = _proj_nat(x2, mix_norm_pre[l], w_nat, w_kidx, sgu_ln_g[l], sgu_ln_b[l])
        qt, vt, qit, wit = _proj_t(x2.reshape(b, s, d), mix_norm_pre[l], wt_cat, wwt)
        yb = _attention(kidx.reshape(b, s, IDX_DIM), qit, wit, k.reshape(b, s, att_w), vt, qt, bias, top_k)
        x2 = _merge(x2, u, vln, yb.reshape(b * s, att_w), gate,
                    sgu_w_s[l], sgu_b[l].reshape(SGU_GROUPS, SGU_CHUNK, 1),
                    w_branch_a[l].astype(BF16), w_branch_b[l].astype(BF16), w_out[l].astype(BF16),
                    mix_norm_post[l])

        x2 = _ffn(x2, ffn2_norm_pre[l], ffn2_norm_post[l],
                  ffn2_w_in[l].astype(BF16), ffn2_w_out[l].astype(BF16))
    return x2.reshape(b, s, d)
```

```python
import functools
import math

import jax
import jax.numpy as jnp
from jax import lax
from jax.experimental import pallas as pl
from jax.experimental.pallas import tpu as pltpu

F32 = jnp.float32
BF16 = jnp.bfloat16
I32 = jnp.int32

SGU_GROUPS = 8
SGU_CHUNK = 128
ATT_HEADS = 8
ATT_HEAD_DIM = 128
IDX_HEADS = 16
IDX_DIM = 64
TOPK_MAX = 256
NUM_BUCKETS = 32
MAX_DISTANCE = 128
NORM_EPS = 1e-6
LN_EPS = 1e-5

V7X_VMEM_BYTES = 64 * 1024 * 1024
VMEM_LIMIT = V7X_VMEM_BYTES - 8 * 1024 * 1024
SUBLANES = 8

FFN_TM = 512
FFN_TF = 512
PROJ_TM = 512
PROJ_BN = 1024
MERGE_TM = 256
ATT_TQ = 256
ATT_KC = 128
ATT_KU = 256

INT_MIN = -(2 ** 31)
MASKED_LOGIT = -1e30
LOG2E = math.log2(math.e)


def _rms(xf, g):
    ms = jnp.mean(xf * xf, axis=-1, keepdims=True)
    return xf * lax.rsqrt(ms + NORM_EPS) * g


def _gelu_tanh(x):
    c = math.sqrt(2.0 / math.pi)
    return x * (0.5 * (1.0 + jnp.tanh(c * (x + 0.044715 * (x * x * x)))))


def _dot(a, b):
    return jnp.dot(a, b, preferred_element_type=F32)


def _dot_nt(a, b):
    return lax.dot_general(a, b, (((1,), (1,)), ((), ())), preferred_element_type=F32)


def _ffn_body(x_ref, gpre_ref, gpost_ref, wa_ref, wb_ref, wo_ref, o_ref, h_scr, acc_scr):
    j = pl.program_id(1)

    @pl.when(j == 0)
    def _():
        h_scr[...] = _rms(x_ref[...], gpre_ref[...]).astype(BF16)
        acc_scr[...] = jnp.zeros_like(acc_scr)

    h = h_scr[...]
    a = _dot(h, wa_ref[...])
    b = _dot(h, wb_ref[...])
    g = (a * jax.nn.sigmoid(a) * b).astype(BF16)
    acc_scr[...] += _dot(g, wo_ref[...])

    @pl.when(j == pl.num_programs(1) - 1)
    def _():
        o_ref[...] = x_ref[...] + 0.5 * _rms(acc_scr[...], gpost_ref[...])


def _ffn(x2, g_pre, g_post, w_in, w_out):
    n, d = x2.shape
    d_ff = w_out.shape[0]
    nf = d_ff // FFN_TF
    return pl.pallas_call(
        _ffn_body,
        grid=(n // FFN_TM, nf),
        in_specs=[
            pl.BlockSpec((FFN_TM, d), lambda i, j: (i, 0)),
            pl.BlockSpec((1, d), lambda i, j: (0, 0)),
            pl.BlockSpec((1, d), lambda i, j: (0, 0)),
            pl.BlockSpec((d, FFN_TF), lambda i, j: (0, j)),
            pl.BlockSpec((d, FFN_TF), lambda i, j: (0, j + nf)),
            pl.BlockSpec((FFN_TF, d), lambda i, j: (j, 0)),
        ],
        out_specs=pl.BlockSpec((FFN_TM, d), lambda i, j: (i, 0)),
        out_shape=jax.ShapeDtypeStruct((n, d), F32),
        scratch_shapes=[pltpu.VMEM((FFN_TM, d), BF16), pltpu.VMEM((FFN_TM, d), F32)],
        compiler_params=pltpu.CompilerParams(
            dimension_semantics=("parallel", "arbitrary"), vmem_limit_bytes=VMEM_LIMIT),
        name="ffn",
    )(x2, g_pre.reshape(1, d), g_post.reshape(1, d), w_in, w_in, w_out)


def _proj_nat_body(x_ref, g_ref, w_ref, wki_ref, lng_ref, lnb_ref,
                   u_ref, vln_ref, k_ref, gate_ref, kidx_ref, h_scr):
    j = pl.program_id(1)

    @pl.when(j == 0)
    def _():
        h = _rms(x_ref[...], g_ref[...]).astype(BF16)
        h_scr[...] = h
        kidx_ref[...] = _dot(h, wki_ref[...]).astype(BF16)

    z = _dot(h_scr[...], w_ref[...])

    @pl.when(j == 0)
    def _():
        u_ref[...] = _gelu_tanh(z).astype(BF16)

    @pl.when(j == 1)
    def _():
        v = _gelu_tanh(z)
        mu = jnp.mean(v, axis=-1, keepdims=True)
        vc = v - mu
        var = jnp.mean(vc * vc, axis=-1, keepdims=True)
        vln_ref[...] = (vc * lax.rsqrt(var + LN_EPS) * lng_ref[...] + lnb_ref[...]).astype(BF16)

    @pl.when(j == 2)
    def _():
        k_ref[...] = z.astype(BF16)

    @pl.when(j >= 3)
    def _():
        gate_ref[...] = jax.nn.sigmoid(z).astype(BF16)


def _proj_nat(x2, g, w_cat, w_kidx, ln_g, ln_b):
    n, d = x2.shape
    ncol = w_cat.shape[1]
    width = PROJ_BN
    nj = ncol // width
    ngate = ncol - 3 * width
    tok = lambda i, j: (i, 0)
    return pl.pallas_call(
        _proj_nat_body,
        grid=(n // PROJ_TM, nj),
        in_specs=[
            pl.BlockSpec((PROJ_TM, d), tok),
            pl.BlockSpec((1, d), lambda i, j: (0, 0)),
            pl.BlockSpec((d, width), lambda i, j: (0, j)),
            pl.BlockSpec((d, IDX_DIM), lambda i, j: (0, 0)),
            pl.BlockSpec((1, width), lambda i, j: (0, 0)),
            pl.BlockSpec((1, width), lambda i, j: (0, 0)),
        ],
        out_specs=[
            pl.BlockSpec((PROJ_TM, width), tok),
            pl.BlockSpec((PROJ_TM, width), tok),
            pl.BlockSpec((PROJ_TM, width), tok),
            pl.BlockSpec((PROJ_TM, width), lambda i, j: (i, jnp.maximum(j - 3, 0))),
            pl.BlockSpec((PROJ_TM, IDX_DIM), tok),
        ],
        out_shape=[
            jax.ShapeDtypeStruct((n, width), BF16),
            jax.ShapeDtypeStruct((n, width), BF16),
            jax.ShapeDtypeStruct((n, width), BF16),
            jax.ShapeDtypeStruct((n, ngate), BF16),
            jax.ShapeDtypeStruct((n, IDX_DIM), BF16),
        ],
        scratch_shapes=[pltpu.VMEM((PROJ_TM, d), BF16)],
        compiler_params=pltpu.CompilerParams(
            dimension_semantics=("parallel", "arbitrary"), vmem_limit_bytes=VMEM_LIMIT),
        name="proj_nat",
    )(x2, g.reshape(1, d), w_cat, w_kidx, ln_g.reshape(1, width), ln_b.reshape(1, width))


def _proj_t_body(x_ref, g_ref, wt_ref, wwt_ref, qt_ref, vt_ref, qit_ref, wit_ref, h_scr):
    j = pl.program_id(2)

    @pl.when(j == 0)
    def _():
        h = _rms(x_ref[0], g_ref[...]).astype(BF16)
        h_scr[...] = h
        wit_ref[0] = _dot_nt(wwt_ref[...], h)

    zt = _dot_nt(wt_ref[...], h_scr[...])

    @pl.when(j == 0)
    def _():
        qt_ref[0] = (zt * (ATT_HEAD_DIM ** -0.5 * LOG2E)).astype(BF16)

    @pl.when(j == 1)
    def _():
        for c in range(PROJ_TM // ATT_KU):
            vt_ref[0, c] = zt[:, c * ATT_KU:(c + 1) * ATT_KU].astype(BF16)

    @pl.when(j == 2)
    def _():
        qit_ref[0] = zt.astype(BF16)


def _proj_t(x3, g, wt_cat, wwt):
    b, s, d = x3.shape
    width = PROJ_BN
    nch = PROJ_TM // ATT_KU
    feat = lambda bi, si, j: (bi, 0, si)
    return pl.pallas_call(
        _proj_t_body,
        grid=(b, s // PROJ_TM, 3),
        in_specs=[
            pl.BlockSpec((1, PROJ_TM, d), lambda bi, si, j: (bi, si, 0)),
            pl.BlockSpec((1, d), lambda bi, si, j: (0, 0)),
            pl.BlockSpec((width, d), lambda bi, si, j: (j, 0)),
            pl.BlockSpec((IDX_HEADS, d), lambda bi, si, j: (0, 0)),
        ],
        out_specs=[
            pl.BlockSpec((1, width, PROJ_TM), feat),
            pl.BlockSpec((1, nch, width, ATT_KU), lambda bi, si, j: (bi, si, 0, 0)),
            pl.BlockSpec((1, width, PROJ_TM), feat),
            pl.BlockSpec((1, IDX_HEADS, PROJ_TM), feat),
        ],
        out_shape=[
            jax.ShapeDtypeStruct((b, width, s), BF16),
            jax.ShapeDtypeStruct((b, s // ATT_KU, width, ATT_KU), BF16),
            jax.ShapeDtypeStruct((b, width, s), BF16),
            jax.ShapeDtypeStruct((b, IDX_HEADS, s), F32),
        ],
        scratch_shapes=[pltpu.VMEM((PROJ_TM, d), BF16)],
        compiler_params=pltpu.CompilerParams(
            dimension_semantics=("parallel", "parallel", "arbitrary"), vmem_limit_bytes=VMEM_LIMIT),
        name="proj_t",
    )(x3, g.reshape(1, d), wt_cat, wwt)


N_NEAR_UNITS = 2


def _bias_body(rb_ref, o_ref):
    h = pl.program_id(0)
    r = lax.broadcasted_iota(I32, (ATT_KU, ATT_TQ), 0)
    c = lax.broadcasted_iota(I32, (ATT_KU, ATT_TQ), 1)
    max_exact = NUM_BUCKETS // 2
    far = rb_ref[NUM_BUCKETS - 1, h]
    for t in range(N_NEAR_UNITS):
        dist = jnp.maximum(t * ATT_KU + c - r, 0)
        nf = jnp.maximum(dist, 1).astype(F32)
        large = max_exact + (jnp.log(nf / max_exact) / math.log(MAX_DISTANCE / max_exact)
                             * (NUM_BUCKETS - max_exact)).astype(I32)
        large = jnp.minimum(large, NUM_BUCKETS - 1)
        bucket = jnp.where(dist < max_exact, dist, large)
        val = jnp.zeros((ATT_KU, ATT_TQ), F32)
        for bkt in range(NUM_BUCKETS):
            val = jnp.where(bucket == bkt, rb_ref[bkt, h], val)
        o_ref[0, t] = (val - far) * LOG2E


def _bias_tiles(rel_bias):
    assert N_NEAR_UNITS * ATT_KU - (ATT_KU - 1) >= MAX_DISTANCE
    return pl.pallas_call(
        _bias_body,
        grid=(ATT_HEADS,),
        in_specs=[pl.BlockSpec(memory_space=pltpu.SMEM)],
        out_specs=pl.BlockSpec((1, N_NEAR_UNITS, ATT_KU, ATT_TQ), lambda h: (h, 0, 0, 0)),
        out_shape=jax.ShapeDtypeStruct((ATT_HEADS, N_NEAR_UNITS, ATT_KU, ATT_TQ), F32),
        name="bias_tiles",
    )(rel_bias)


def _sublane_allmax(x):
    for shift in (4, 2, 1):
        x = jnp.maximum(x, pltpu.roll(x, shift, axis=0))
    return x


def _attn_body(kidx_ref, qit_ref, wit_ref, k_ref, vt_ref, qt_ref, bias_ref, o_ref,
               key_scr, m_scr, l_scr, acc_scr, *, top_k):
    i = pl.program_id(1)
    kc, ku, tq, sl = ATT_KC, ATT_KU, ATT_TQ, SUBLANES
    n_units = i + 1
    q0 = i * tq

    w_all = wit_ref[0] * (IDX_HEADS ** -0.5 * IDX_DIM ** -0.5)
    row = lax.broadcasted_iota(I32, (kc, tq), 0)
    col = lax.broadcasted_iota(I32, (kc, tq), 1)

    def score_chunk(c, carry):
        r0 = pl.multiple_of(c * kc, kc)
        kch = kidx_ref[0, pl.ds(r0, kc), :]
        acc = jnp.zeros((kc, tq), F32)
        for h in range(IDX_HEADS):
            d = _dot(kch, qit_ref[0, h * IDX_DIM:(h + 1) * IDX_DIM, :])
            acc = acc + jnp.maximum(d, 0.0) * w_all[h:h + 1, :]
        bits = lax.bitcast_convert_type(acc, I32)
        key = jnp.where(bits < 0, bits ^ jnp.int32(0x7FFFFFFF), bits)
        key = jnp.where(r0 + row <= q0 + col, key, INT_MIN)
        key_scr[pl.ds(r0, kc), :] = key
        return carry

    lax.fori_loop(0, n_units * (ku // kc), score_chunk, 0)

    def count_ge(cand):
        def body(u, cnt):
            r0 = pl.multiple_of(u * ku, ku)
            blk = key_scr[pl.ds(r0, ku), :].reshape(ku // sl, sl, tq)
            return cnt + jnp.sum(jnp.where(blk >= cand[None], 1, 0), axis=0)
        cnt8 = lax.fori_loop(0, n_units, body, jnp.zeros((sl, tq), I32))
        return jnp.sum(cnt8, axis=0, keepdims=True)

    zero = jnp.zeros((1, tq), I32)
    prefix = jnp.where(count_ge(zero) >= top_k, zero, INT_MIN)

    def descend(t, prefix):
        cand = prefix | jnp.left_shift(jnp.int32(1), 30 - t)
        return jnp.where(count_ge(cand) >= top_k, cand, prefix)

    prefix = lax.fori_loop(0, 31, descend, prefix)
    thr = jnp.maximum(prefix, INT_MIN + 1)

    m_scr[...] = jnp.full(m_scr.shape, MASKED_LOGIT, F32)
    l_scr[...] = jnp.zeros(l_scr.shape, F32)
    acc_scr[...] = jnp.zeros(acc_scr.shape, F32)

    def unit(u, near_tile):
        r0 = pl.multiple_of(u * ku, ku)
        keep = key_scr[pl.ds(r0, ku), :].reshape(ku // sl, sl, tq) >= thr[None]
        mask_add = jnp.where(keep, 0.0, MASKED_LOGIT)
        def qk(h):
            hs = slice(h * ATT_HEAD_DIM, (h + 1) * ATT_HEAD_DIM)
            s = _dot(k_ref[0, pl.ds(r0, ku), hs], qt_ref[0, hs, :])
            if near_tile is not None:
                s = s + bias_ref[h, near_tile]
            return s

        s_next = qk(0)
        for h in range(ATT_HEADS):
            hs = slice(h * ATT_HEAD_DIM, (h + 1) * ATT_HEAD_DIM)
            s = s_next
            if h + 1 < ATT_HEADS:
                s_next = qk(h + 1)
            s = s.reshape(ku // sl, sl, tq) + mask_add
            m_old = m_scr[h]
            m_new = jnp.maximum(m_old, _sublane_allmax(jnp.max(s, axis=0)))
            alpha = jnp.exp2(m_old - m_new)
            p = jnp.exp2(s - m_new[None])
            m_scr[h] = m_new
            l_scr[h] = l_scr[h] * alpha + jnp.sum(p, axis=0)
            pv = _dot(vt_ref[0, u, hs, :], p.reshape(ku, tq).astype(BF16))
            acc = acc_scr[h].reshape(ATT_HEAD_DIM // sl, sl, tq) * alpha[None]
            acc_scr[h] = acc.reshape(ATT_HEAD_DIM, tq) + pv

    lax.fori_loop(0, n_units - N_NEAR_UNITS, lambda u, c: (unit(u, None), c)[1], 0)
    for t in range(N_NEAR_UNITS - 1, -1, -1):
        lax.fori_loop(0, jnp.minimum(n_units - t, 1), lambda _, c, t=t: (unit(i - t, t), c)[1], 0)

    for h in range(ATT_HEADS):
        hs = slice(h * ATT_HEAD_DIM, (h + 1) * ATT_HEAD_DIM)
        l = jnp.sum(l_scr[h], axis=0, keepdims=True)
        o_ref[0, :, hs] = (acc_scr[h] / l).T.astype(BF16)


def _attention(kidx, qit, wit, k, vt, qt, bias, top_k):
    b, s, width = k.shape
    one = pl.Buffered(1)
    return pl.pallas_call(
        functools.partial(_attn_body, top_k=top_k),
        grid=(b, s // ATT_TQ),
        in_specs=[
            pl.BlockSpec((1, s, IDX_DIM), lambda bi, i: (bi, 0, 0), pipeline_mode=one),
            pl.BlockSpec((1, IDX_HEADS * IDX_DIM, ATT_TQ), lambda bi, i: (bi, 0, i)),
            pl.BlockSpec((1, IDX_HEADS, ATT_TQ), lambda bi, i: (bi, 0, i)),
            pl.BlockSpec((1, s, width), lambda bi, i: (bi, 0, 0), pipeline_mode=one),
            pl.BlockSpec((1, s // ATT_KU, width, ATT_KU), lambda bi, i: (bi, 0, 0, 0), pipeline_mode=one),
            pl.BlockSpec((1, width, ATT_TQ), lambda bi, i: (bi, 0, i)),
            pl.BlockSpec(bias.shape, lambda bi, i: (0, 0, 0, 0), pipeline_mode=one),
        ],
        out_specs=pl.BlockSpec((1, ATT_TQ, width), lambda bi, i: (bi, i, 0)),
        out_shape=jax.ShapeDtypeStruct((b, s, width), BF16),
        scratch_shapes=[
            pltpu.VMEM((s, ATT_TQ), I32),
            pltpu.VMEM((ATT_HEADS, SUBLANES, ATT_TQ), F32),
            pltpu.VMEM((ATT_HEADS, SUBLANES, ATT_TQ), F32),
            pltpu.VMEM((ATT_HEADS, ATT_HEAD_DIM, ATT_TQ), F32),
        ],
        compiler_params=pltpu.CompilerParams(
            dimension_semantics=("parallel", "arbitrary"), vmem_limit_bytes=VMEM_LIMIT),
        name="dsa_attention",
    )(kidx, qit, wit, k, vt, qt, bias)


def _merge_body(x_ref, u_ref, vln_ref, yb_ref, gate_ref, ws_ref, bs_ref, wa_ref, wb_ref, wo_ref,
                gpost_ref, o_ref, ya_scr):
    ch = SGU_CHUNK
    d = x_ref.shape[1]
    tril = (lax.broadcasted_iota(I32, (ch, ch), 0) >= lax.broadcasted_iota(I32, (ch, ch), 1))
    for g in range(SGU_GROUPS):
        gs = slice(g * ch, (g + 1) * ch)
        wsg = jnp.where(tril, ws_ref[g], 0.0).astype(BF16)
        for c in range(MERGE_TM // ch):
            cs = slice(c * ch, (c + 1) * ch)
            mixed = _dot(wsg, vln_ref[cs, gs]) + bs_ref[g]
            ya_scr[cs, gs] = (u_ref[cs, gs].astype(F32) * mixed).astype(BF16)
    ma = _dot(ya_scr[...], wa_ref[...])
    mb = _dot(yb_ref[...], wb_ref[...])
    merged = gate_ref[:, :d].astype(F32) * ma + gate_ref[:, d:].astype(F32) * mb
    o = _dot(merged.astype(BF16), wo_ref[...])
    o_ref[...] = x_ref[...] + _rms(o, gpost_ref[...])


def _merge(x2, u, vln, yb, gate, w_s, b_s, w_a, w_b, w_o, g_post):
    n, d = x2.shape
    width = u.shape[1]
    one = pl.Buffered(1)
    tok = lambda i: (i, 0)
    const2 = lambda i: (0, 0)
    const3 = lambda i: (0, 0, 0)
    return pl.pallas_call(
        _merge_body,
        grid=(n // MERGE_TM,),
        in_specs=[
            pl.BlockSpec((MERGE_TM, d), tok),
            pl.BlockSpec((MERGE_TM, width), tok),
            pl.BlockSpec((MERGE_TM, width), tok),
            pl.BlockSpec((MERGE_TM, width), tok),
            pl.BlockSpec((MERGE_TM, 2 * d), tok),
            pl.BlockSpec(w_s.shape, const3, pipeline_mode=one),
            pl.BlockSpec(b_s.shape, const3, pipeline_mode=one),
            pl.BlockSpec(w_a.shape, const2, pipeline_mode=one),
            pl.BlockSpec(w_b.shape, const2, pipeline_mode=one),
            pl.BlockSpec(w_o.shape, const2, pipeline_mode=one),
            pl.BlockSpec((1, d), const2),
        ],
        out_specs=pl.BlockSpec((MERGE_TM, d), tok),
        out_shape=jax.ShapeDtypeStruct((n, d), F32),
        scratch_shapes=[pltpu.VMEM((MERGE_TM, width), BF16)],
        compiler_params=pltpu.CompilerParams(
            dimension_semantics=("parallel",), vmem_limit_bytes=VMEM_LIMIT),
        name="merge",
    )(x2, u, vln, yb, gate, w_s, b_s, w_a, w_b, w_o, g_post.reshape(1, d))


def kernel(x, ffn1_norm_pre, ffn1_norm_post, ffn1_w_in, ffn1_w_out, mix_norm_pre, mix_norm_post, w_in,
           sgu_ln_g, sgu_ln_b, sgu_w_s, sgu_b, rel_bias, w_branch_a, w_branch_b, w_gate, w_out,
           ffn2_norm_pre, ffn2_norm_post, ffn2_w_in, ffn2_w_out):
    b, s, d = x.shape
    depth = w_in.shape[0]
    sgu_w = sgu_ln_g.shape[1]
    att_w = ATT_HEADS * ATT_HEAD_DIM
    idx_w = IDX_HEADS * IDX_DIM
    top_k = min(TOPK_MAX, s // 4)
    assert s % PROJ_TM == 0 and s % ATT_TQ == 0 and (b * s) % FFN_TM == 0
    assert sgu_w == PROJ_BN and att_w == PROJ_BN and idx_w == PROJ_BN and ATT_KU == ATT_TQ

    sizes = (sgu_w, sgu_w, att_w, att_w, att_w, idx_w, IDX_DIM, IDX_HEADS)
    offs = [0]
    for sz in sizes:
        offs.append(offs[-1] + sz)
    col = lambda w, idx: w[:, offs[idx]:offs[idx + 1]]

    bias = _bias_tiles(rel_bias)
    x2 = x.reshape(b * s, d)
    for l in range(depth):
        wl = w_in[l]
        w_nat = jnp.concatenate([col(wl, 0), col(wl, 1), col(wl, 3), w_gate[l]], axis=1).astype(BF16)
        w_kidx = col(wl, 6).astype(BF16)
        wt_cat = jnp.concatenate([col(wl, 2), col(wl, 4), col(wl, 5)], axis=1).T.astype(BF16)
        wwt = col(wl, 7).T.astype(BF16)

        x2 = _ffn(x2, ffn1_norm_pre[l], ffn1_norm_post[l],
                  ffn1_w_in[l].astype(BF16), ffn1_w_out[l].astype(BF16))

        u, vln, k, gate, kidx = _proj_nat(x2, mix_norm_pre[l], w_nat, w_kidx, sgu_ln_g[l], sgu_ln_b[l])
        qt, vt, qit, wit = _proj_t(x2.reshape(b, s, d), mix_norm_pre[l], wt_cat, wwt)
        yb = _attention(kidx.reshape(b, s, IDX_DIM), qit, wit, k.reshape(b, s, att_w), vt, qt, bias, top_k)
        x2 = _merge(x2, u, vln, yb.reshape(b * s, att_w), gate,
                    sgu_w_s[l], sgu_b[l].reshape(SGU_GROUPS, SGU_CHUNK, 1),
                    w_branch_a[l].astype(BF16), w_branch_b[l].astype(BF16), w_out[l].astype(BF16),
                    mix_norm_post[l])

        x2 = _ffn(x2, ffn2_norm_pre[l], ffn2_norm_post[l],
                  ffn2_w_in[l].astype(BF16), ffn2_w_out[l].astype(BF16))
    return x2.reshape(b, s, d)
```

```python
import functools
import math

import jax
import jax.numpy as jnp
from jax import lax
from jax.experimental import pallas as pl
from jax.experimental.pallas import tpu as pltpu

F32 = jnp.float32
BF16 = jnp.bfloat16
I32 = jnp.int32

SGU_GROUPS = 8
SGU_CHUNK = 128
ATT_HEADS = 8
ATT_HEAD_DIM = 128
IDX_HEADS = 16
IDX_DIM = 64
TOPK_MAX = 256
NUM_BUCKETS = 32
MAX_DISTANCE = 128
NORM_EPS = 1e-6
LN_EPS = 1e-5

V7X_VMEM_BYTES = 64 * 1024 * 1024
VMEM_LIMIT = V7X_VMEM_BYTES - 8 * 1024 * 1024
SUBLANES = 8
BF16_TILE_ROWS = 16

FFN_TM = 512
FFN_TF = 512
PROJ_TM = 512
PROJ_BN = 1024
PROJ_CH = 256
MERGE_TM = 256
ATT_TQ = 256
ATT_KU = 256
ACC_ROWS = ATT_HEAD_DIM + BF16_TILE_ROWS
QK_AHEAD = ATT_HEADS

INT_MIN = -(2 ** 31)
MASKED_LOGIT = -1e30
LOG2E = math.log2(math.e)


def _rms(xf, g):
    ms = jnp.mean(xf * xf, axis=-1, keepdims=True)
    return xf * lax.rsqrt(ms + NORM_EPS) * g


def _gelu_tanh(x):
    c = math.sqrt(2.0 / math.pi)
    return x * (0.5 * (1.0 + jnp.tanh(c * (x + 0.044715 * (x * x * x)))))


def _dot(a, b):
    return jnp.dot(a, b, preferred_element_type=F32)


def _dot_nt(a, b):
    return lax.dot_general(a, b, (((1,), (1,)), ((), ())), preferred_element_type=F32)


def _ffn_body(*refs, emit_next):
    if emit_next:
        x_ref, gpre_ref, gpost_ref, gnext_ref, wa_ref, wb_ref, wo_ref, o_ref, hn_ref, h_scr, acc_scr = refs
    else:
        x_ref, gpre_ref, gpost_ref, wa_ref, wb_ref, wo_ref, o_ref, h_scr, acc_scr = refs
    j = pl.program_id(1)

    @pl.when(j == 0)
    def _():
        h_scr[...] = _rms(x_ref[...], gpre_ref[...]).astype(BF16)
        acc_scr[...] = jnp.zeros_like(acc_scr)

    h = h_scr[...]
    a = _dot(h, wa_ref[...])
    b = _dot(h, wb_ref[...])
    g = (a * jax.nn.sigmoid(a) * b).astype(BF16)
    acc_scr[...] += _dot(g, wo_ref[...])

    @pl.when(j == pl.num_programs(1) - 1)
    def _():
        y = x_ref[...] + 0.5 * _rms(acc_scr[...], gpost_ref[...])
        o_ref[...] = y
        if emit_next:
            hn_ref[...] = _rms(y, gnext_ref[...]).astype(BF16)


def _ffn(x2, g_pre, g_post, w_in, w_out, g_next=None):
    n, d = x2.shape
    d_ff = w_out.shape[0]
    nf = d_ff // FFN_TF
    emit_next = g_next is not None
    vec = pl.BlockSpec((1, d), lambda i, j: (0, 0))
    tok = pl.BlockSpec((FFN_TM, d), lambda i, j: (i, 0))
    gains = [g_pre, g_post] + ([g_next] if emit_next else [])
    out_shape = [jax.ShapeDtypeStruct((n, d), F32)] + ([jax.ShapeDtypeStruct((n, d), BF16)] if emit_next else [])
    return pl.pallas_call(
        functools.partial(_ffn_body, emit_next=emit_next),
        grid=(n // FFN_TM, nf),
        in_specs=[tok] + [vec] * len(gains) + [
            pl.BlockSpec((d, FFN_TF), lambda i, j: (0, j)),
            pl.BlockSpec((d, FFN_TF), lambda i, j: (0, j + nf)),
            pl.BlockSpec((FFN_TF, d), lambda i, j: (j, 0)),
        ],
        out_specs=[tok] * len(out_shape),
        out_shape=out_shape,
        scratch_shapes=[pltpu.VMEM((FFN_TM, d), BF16), pltpu.VMEM((FFN_TM, d), F32)],
        compiler_params=pltpu.CompilerParams(
            dimension_semantics=("parallel", "arbitrary"), vmem_limit_bytes=VMEM_LIMIT),
        name="ffn",
    )(x2, *[g.reshape(1, d) for g in gains], w_in, w_in, w_out)


def _chunked(n_chunks, matmul, epilogue):
    z = [matmul(0)]
    for c in range(n_chunks):
        if c + 1 < n_chunks:
            z.append(matmul(c + 1))
        epilogue(c, z[c])
        z[c] = None


def _proj_nat_body(h_ref, w_ref, wki_ref, lng_ref, lnb_ref,
                   u_ref, vln_ref, k_ref, gate_ref, kidx_ref, v_scr):
    j = pl.program_id(1)
    ch = PROJ_CH
    nc = PROJ_BN // ch
    cols = lambda c: slice(c * ch, (c + 1) * ch)
    matmul = lambda c: _dot(h_ref[...], w_ref[:, cols(c)])

    @pl.when(j == 0)
    def _():
        kidx_ref[...] = _dot(h_ref[...], wki_ref[...]).astype(BF16)

        def store_u(c, z):
            u_ref[:, cols(c)] = _gelu_tanh(z).astype(BF16)
        _chunked(nc, matmul, store_u)

    @pl.when(j == 1)
    def _():
        def store_v(c, z):
            v_scr[:, cols(c)] = _gelu_tanh(z)
        _chunked(nc, matmul, store_v)
        v = v_scr[...]
        mu = jnp.mean(v, axis=-1, keepdims=True)
        vc = v - mu
        var = jnp.mean(vc * vc, axis=-1, keepdims=True)
        vln_ref[...] = (vc * lax.rsqrt(var + LN_EPS) * lng_ref[...] + lnb_ref[...]).astype(BF16)

    @pl.when(j == 2)
    def _():
        def store_k(c, z):
            k_ref[:, cols(c)] = z.astype(BF16)
        _chunked(nc, matmul, store_k)

    @pl.when(j >= 3)
    def _():
        def store_gate(c, z):
            gate_ref[:, cols(c)] = jax.nn.sigmoid(z).astype(BF16)
        _chunked(nc, matmul, store_gate)


def _proj_nat(h2, w_cat, w_kidx, ln_g, ln_b):
    n, d = h2.shape
    ncol = w_cat.shape[1]
    width = PROJ_BN
    nj = ncol // width
    ngate = ncol - 3 * width
    tok = lambda i, j: (i, 0)
    return pl.pallas_call(
        _proj_nat_body,
        grid=(n // PROJ_TM, nj),
        in_specs=[
            pl.BlockSpec((PROJ_TM, d), tok),
            pl.BlockSpec((d, width), lambda i, j: (0, j)),
            pl.BlockSpec((d, IDX_DIM), lambda i, j: (0, 0)),
            pl.BlockSpec((1, width), lambda i, j: (0, 0)),
            pl.BlockSpec((1, width), lambda i, j: (0, 0)),
        ],
        out_specs=[
            pl.BlockSpec((PROJ_TM, width), tok),
            pl.BlockSpec((PROJ_TM, width), tok),
            pl.BlockSpec((PROJ_TM, width), tok),
            pl.BlockSpec((PROJ_TM, width), lambda i, j: (i, jnp.maximum(j - 3, 0))),
            pl.BlockSpec((PROJ_TM, IDX_DIM), tok),
        ],
        out_shape=[
            jax.ShapeDtypeStruct((n, width), BF16),
            jax.ShapeDtypeStruct((n, width), BF16),
            jax.ShapeDtypeStruct((n, width), BF16),
            jax.ShapeDtypeStruct((n, ngate), BF16),
            jax.ShapeDtypeStruct((n, IDX_DIM), BF16),
        ],
        scratch_shapes=[pltpu.VMEM((PROJ_TM, width), F32)],
        compiler_params=pltpu.CompilerParams(
            dimension_semantics=("parallel", "arbitrary"), vmem_limit_bytes=VMEM_LIMIT),
        name="proj_nat",
    )(h2, w_cat, w_kidx, ln_g.reshape(1, width), ln_b.reshape(1, width))


def _proj_t_body(h_ref, wt_ref, wwt_ref, qt_ref, vt_ref, qit_ref, wit_ref):
    j = pl.program_id(2)
    ch = PROJ_CH
    nc = PROJ_BN // ch
    rows = lambda c: slice(c * ch, (c + 1) * ch)
    matmul = lambda c: _dot_nt(wt_ref[rows(c), :], h_ref[0])

    @pl.when(j == 0)
    def _():
        wit_ref[0] = _dot_nt(wwt_ref[...], h_ref[0])

        def store_q(c, zt):
            qt_ref[0, rows(c), :] = (zt * (ATT_HEAD_DIM ** -0.5 * LOG2E)).astype(BF16)
        _chunked(nc, matmul, store_q)

    @pl.when(j == 1)
    def _():
        def store_v(c, zt):
            for cc in range(PROJ_TM // ATT_KU):
                vt_ref[0, cc, rows(c), :] = zt[:, cc * ATT_KU:(cc + 1) * ATT_KU].astype(BF16)
        _chunked(nc, matmul, store_v)

    @pl.when(j == 2)
    def _():
        def store_qi(c, zt):
            qit_ref[0, rows(c), :] = zt.astype(BF16)
        _chunked(nc, matmul, store_qi)


def _proj_t(h3, wt_cat, wwt):
    b, s, d = h3.shape
    width = PROJ_BN
    nch = PROJ_TM // ATT_KU
    feat = lambda bi, si, j: (bi, 0, si)
    return pl.pallas_call(
        _proj_t_body,
        grid=(b, s // PROJ_TM, 3),
        in_specs=[
            pl.BlockSpec((1, PROJ_TM, d), lambda bi, si, j: (bi, si, 0)),
            pl.BlockSpec((width, d), lambda bi, si, j: (j, 0)),
            pl.BlockSpec((IDX_HEADS, d), lambda bi, si, j: (0, 0)),
        ],
        out_specs=[
            pl.BlockSpec((1, width, PROJ_TM), feat),
            pl.BlockSpec((1, nch, width, ATT_KU), lambda bi, si, j: (bi, si, 0, 0)),
            pl.BlockSpec((1, width, PROJ_TM), feat),
            pl.BlockSpec((1, IDX_HEADS, PROJ_TM), feat),
        ],
        out_shape=[
            jax.ShapeDtypeStruct((b, width, s), BF16),
            jax.ShapeDtypeStruct((b, s // ATT_KU, width, ATT_KU), BF16),
            jax.ShapeDtypeStruct((b, width, s), BF16),
            jax.ShapeDtypeStruct((b, IDX_HEADS, s), F32),
        ],
        compiler_params=pltpu.CompilerParams(
            dimension_semantics=("parallel", "parallel", "arbitrary"), vmem_limit_bytes=VMEM_LIMIT),
        name="proj_t",
    )(h3, wt_cat, wwt)


N_NEAR_UNITS = 2


def _bias_body(rb_ref, o_ref):
    h = pl.program_id(0)
    r = lax.broadcasted_iota(I32, (ATT_KU, ATT_TQ), 0)
    c = lax.broadcasted_iota(I32, (ATT_KU, ATT_TQ), 1)
    max_exact = NUM_BUCKETS // 2
    far = rb_ref[NUM_BUCKETS - 1, h]
    for t in range(N_NEAR_UNITS):
        dist = jnp.maximum(t * ATT_KU + c - r, 0)
        nf = jnp.maximum(dist, 1).astype(F32)
        large = max_exact + (jnp.log(nf / max_exact) / math.log(MAX_DISTANCE / max_exact)
                             * (NUM_BUCKETS - max_exact)).astype(I32)
        large = jnp.minimum(large, NUM_BUCKETS - 1)
        bucket = jnp.where(dist < max_exact, dist, large)
        val = jnp.zeros((ATT_KU, ATT_TQ), F32)
        for bkt in range(NUM_BUCKETS):
            val = jnp.where(bucket == bkt, rb_ref[bkt, h], val)
        o_ref[0, t] = (val - far) * LOG2E


def _bias_tiles(rel_bias):
    assert N_NEAR_UNITS * ATT_KU - (ATT_KU - 1) >= MAX_DISTANCE
    return pl.pallas_call(
        _bias_body,
        grid=(ATT_HEADS,),
        in_specs=[pl.BlockSpec(memory_space=pltpu.SMEM)],
        out_specs=pl.BlockSpec((1, N_NEAR_UNITS, ATT_KU, ATT_TQ), lambda h: (h, 0, 0, 0)),
        out_shape=jax.ShapeDtypeStruct((ATT_HEADS, N_NEAR_UNITS, ATT_KU, ATT_TQ), F32),
        name="bias_tiles",
    )(rel_bias)


def _sublane_allmax(x):
    for shift in (4, 2, 1):
        x = jnp.maximum(x, pltpu.roll(x, shift, axis=0))
    return x


def _attn_body(kidx_ref, qit_ref, wit_ref, k_ref, vt_ref, qt_ref, bias_ref, o_ref,
               key_scr, m_scr, acc_scr, *, top_k):
    i = pl.program_id(1)
    ku, tq, sl = ATT_KU, ATT_TQ, SUBLANES
    n_units = i + 1
    q0 = i * tq

    w_all = wit_ref[0] * (IDX_HEADS ** -0.5 * IDX_DIM ** -0.5)
    row = lax.broadcasted_iota(I32, (ku, tq), 0)
    col = lax.broadcasted_iota(I32, (ku, tq), 1)

    def score_unit(u, carry):
        r0 = pl.multiple_of(u * ku, ku)
        kch = kidx_ref[0, pl.ds(r0, ku), :]
        acc = jnp.zeros((ku, tq), F32)
        for h in range(IDX_HEADS):
            d = _dot(kch, qit_ref[0, h * IDX_DIM:(h + 1) * IDX_DIM, :])
            acc = acc + jnp.maximum(d, 0.0) * w_all[h:h + 1, :]
        bits = lax.bitcast_convert_type(acc, I32)
        key = jnp.where(bits < 0, bits ^ jnp.int32(0x7FFFFFFF), bits)
        key = jnp.where(r0 + row <= q0 + col, key, INT_MIN)
        key_scr[pl.ds(r0, ku), :] = key
        return carry

    lax.fori_loop(0, n_units, score_unit, 0)

    def count_ge(cand):
        def body(u, cnt):
            r0 = pl.multiple_of(u * ku, ku)
            blk = key_scr[pl.ds(r0, ku), :].reshape(ku // sl, sl, tq)
            return cnt + jnp.sum(jnp.where(blk >= cand[None], 1, 0), axis=0)
        cnt8 = lax.fori_loop(0, n_units, body, jnp.zeros((sl, tq), I32))
        return jnp.sum(cnt8, axis=0, keepdims=True)

    zero = jnp.zeros((1, tq), I32)
    prefix = jnp.where(count_ge(zero) >= top_k, zero, INT_MIN)

    def descend(t, prefix):
        cand = prefix | jnp.left_shift(jnp.int32(1), 30 - t)
        return jnp.where(count_ge(cand) >= top_k, cand, prefix)

    prefix = lax.fori_loop(0, 31, descend, prefix)
    thr = jnp.maximum(prefix, INT_MIN + 1)

    m_scr[...] = jnp.full(m_scr.shape, MASKED_LOGIT, F32)
    acc_scr[...] = jnp.zeros(acc_scr.shape, F32)
    ones_rows = jnp.ones((ACC_ROWS - ATT_HEAD_DIM, ku), BF16)

    def unit(u, near_tile):
        r0 = pl.multiple_of(u * ku, ku)
        keep = key_scr[pl.ds(r0, ku), :].reshape(ku // sl, sl, tq) >= thr[None]
        mask_add = jnp.where(keep, 0.0, MASKED_LOGIT)

        def qk(h):
            hs = slice(h * ATT_HEAD_DIM, (h + 1) * ATT_HEAD_DIM)
            s = _dot(k_ref[0, pl.ds(r0, ku), hs], qt_ref[0, hs, :])
            if near_tile is not None:
                s = s + bias_ref[h, near_tile]
            return s

        pending = [qk(h) for h in range(min(QK_AHEAD, ATT_HEADS))]
        for h in range(ATT_HEADS):
            hs = slice(h * ATT_HEAD_DIM, (h + 1) * ATT_HEAD_DIM)
            s = pending.pop(0)
            if h + QK_AHEAD < ATT_HEADS:
                pending.append(qk(h + QK_AHEAD))
            s = s.reshape(ku // sl, sl, tq) + mask_add
            m_old = m_scr[h]
            m_new = jnp.maximum(m_old, _sublane_allmax(jnp.max(s, axis=0)))
            alpha = jnp.exp2(m_old - m_new)
            p = jnp.exp2(s - m_new[None])
            m_scr[h] = m_new
            v_ext = jnp.concatenate([vt_ref[0, u, hs, :], ones_rows], axis=0)
            pv = _dot(v_ext, p.reshape(ku, tq).astype(BF16))
            acc = acc_scr[h].reshape(ACC_ROWS // sl, sl, tq) * alpha[None]
            acc_scr[h] = pv + acc.reshape(ACC_ROWS, tq)

    lax.fori_loop(0, n_units - N_NEAR_UNITS, lambda u, c: (unit(u, None), c)[1], 0)
    for t in range(N_NEAR_UNITS - 1, -1, -1):
        lax.fori_loop(0, jnp.minimum(n_units - t, 1), lambda _, c, t=t: (unit(i - t, t), c)[1], 0)

    for h in range(ATT_HEADS):
        hs = slice(h * ATT_HEAD_DIM, (h + 1) * ATT_HEAD_DIM)
        l = acc_scr[h, ATT_HEAD_DIM:ATT_HEAD_DIM + 1, :]
        o_ref[0, :, hs] = (acc_scr[h, :ATT_HEAD_DIM, :] / l).T.astype(BF16)


def _attention(kidx, qit, wit, k, vt, qt, bias, top_k):
    b, s, width = k.shape
    one = pl.Buffered(1)
    return pl.pallas_call(
        functools.partial(_attn_body, top_k=top_k),
        grid=(b, s // ATT_TQ),
        in_specs=[
            pl.BlockSpec((1, s, IDX_DIM), lambda bi, i: (bi, 0, 0), pipeline_mode=one),
            pl.BlockSpec((1, IDX_HEADS * IDX_DIM, ATT_TQ), lambda bi, i: (bi, 0, i)),
            pl.BlockSpec((1, IDX_HEADS, ATT_TQ), lambda bi, i: (bi, 0, i)),
            pl.BlockSpec((1, s, width), lambda bi, i: (bi, 0, 0), pipeline_mode=one),
            pl.BlockSpec((1, s // ATT_KU, width, ATT_KU), lambda bi, i: (bi, 0, 0, 0), pipeline_mode=one),
            pl.BlockSpec((1, width, ATT_TQ), lambda bi, i: (bi, 0, i)),
            pl.BlockSpec(bias.shape, lambda bi, i: (0, 0, 0, 0), pipeline_mode=one),
        ],
        out_specs=pl.BlockSpec((1, ATT_TQ, width), lambda bi, i: (bi, i, 0)),
        out_shape=jax.ShapeDtypeStruct((b, s, width), BF16),
        scratch_shapes=[
            pltpu.VMEM((s, ATT_TQ), I32),
            pltpu.VMEM((ATT_HEADS, SUBLANES, ATT_TQ), F32),
            pltpu.VMEM((ATT_HEADS, ACC_ROWS, ATT_TQ), F32),
        ],
        compiler_params=pltpu.CompilerParams(
            dimension_semantics=("parallel", "arbitrary"), vmem_limit_bytes=VMEM_LIMIT),
        name="dsa_attention",
    )(kidx, qit, wit, k, vt, qt, bias)


def _merge_body(x_ref, u_ref, vln_ref, yb_ref, gate_ref, ws_ref, bs_ref, wa_ref, wb_ref, wo_ref,
                gpost_ref, o_ref, ya_scr):
    ch = SGU_CHUNK
    d = x_ref.shape[1]
    tril = (lax.broadcasted_iota(I32, (ch, ch), 0) >= lax.broadcasted_iota(I32, (ch, ch), 1))
    for g in range(SGU_GROUPS):
        gs = slice(g * ch, (g + 1) * ch)
        wsg = jnp.where(tril, ws_ref[g], 0.0).astype(BF16)
        for c in range(MERGE_TM // ch):
            cs = slice(c * ch, (c + 1) * ch)
            mixed = _dot(wsg, vln_ref[cs, gs]) + bs_ref[g]
            ya_scr[cs, gs] = (u_ref[cs, gs].astype(F32) * mixed).astype(BF16)
    ma = _dot(ya_scr[...], wa_ref[...])
    mb = _dot(yb_ref[...], wb_ref[...])
    merged = gate_ref[:, :d].astype(F32) * ma + gate_ref[:, d:].astype(F32) * mb
    o = _dot(merged.astype(BF16), wo_ref[...])
    o_ref[...] = x_ref[...] + _rms(o, gpost_ref[...])


def _merge(x2, u, vln, yb, gate, w_s, b_s, w_a, w_b, w_o, g_post):
    n, d = x2.shape
    width = u.shape[1]
    one = pl.Buffered(1)
    tok = lambda i: (i, 0)
    const2 = lambda i: (0, 0)
    const3 = lambda i: (0, 0, 0)
    return pl.pallas_call(
        _merge_body,
        grid=(n // MERGE_TM,),
        in_specs=[
            pl.BlockSpec((MERGE_TM, d), tok),
            pl.BlockSpec((MERGE_TM, width), tok),
            pl.BlockSpec((MERGE_TM, width), tok),
            pl.BlockSpec((MERGE_TM, width), tok),
            pl.BlockSpec((MERGE_TM, 2 * d), tok),
            pl.BlockSpec(w_s.shape, const3, pipeline_mode=one),
            pl.BlockSpec(b_s.shape, const3, pipeline_mode=one),
            pl.BlockSpec(w_a.shape, const2, pipeline_mode=one),
            pl.BlockSpec(w_b.shape, const2, pipeline_mode=one),
            pl.BlockSpec(w_o.shape, const2, pipeline_mode=one),
            pl.BlockSpec((1, d), const2),
        ],
        out_specs=pl.BlockSpec((MERGE_TM, d), tok),
        out_shape=jax.ShapeDtypeStruct((n, d), F32),
        scratch_shapes=[pltpu.VMEM((MERGE_TM, width), BF16)],
        compiler_params=pltpu.CompilerParams(
            dimension_semantics=("parallel",), vmem_limit_bytes=VMEM_LIMIT),
        name="merge",
    )(x2, u, vln, yb, gate, w_s, b_s, w_a, w_b, w_o, g_post.reshape(1, d))


def kernel(x, ffn1_norm_pre, ffn1_norm_post, ffn1_w_in, ffn1_w_out, mix_norm_pre, mix_norm_post, w_in,
           sgu_ln_g, sgu_ln_b, sgu_w_s, sgu_b, rel_bias, w_branch_a, w_branch_b, w_gate, w_out,
           ffn2_norm_pre, ffn2_norm_post, ffn2_w_in, ffn2_w_out):
    b, s, d = x.shape
    depth = w_in.shape[0]
    sgu_w = sgu_ln_g.shape[1]
    att_w = ATT_HEADS * ATT_HEAD_DIM
    idx_w = IDX_HEADS * IDX_DIM
    top_k = min(TOPK_MAX, s // 4)
    assert s % PROJ_TM == 0 and s % ATT_TQ == 0 and (b * s) % FFN_TM == 0
    assert sgu_w == PROJ_BN and att_w == PROJ_BN and idx_w == PROJ_BN and ATT_KU == ATT_TQ

    sizes = (sgu_w, sgu_w, att_w, att_w, att_w, idx_w, IDX_DIM, IDX_HEADS)
    offs = [0]
    for sz in sizes:
        offs.append(offs[-1] + sz)
    col = lambda w, idx: w[:, offs[idx]:offs[idx + 1]]

    bias = _bias_tiles(rel_bias)
    x2 = x.reshape(b * s, d)
    for l in range(depth):
        wl = w_in[l]
        w_nat = jnp.concatenate([col(wl, 0), col(wl, 1), col(wl, 3), w_gate[l]], axis=1).astype(BF16)
        w_kidx = col(wl, 6).astype(BF16)
        wt_cat = jnp.concatenate([col(wl, 2), col(wl, 4), col(wl, 5)], axis=1).T.astype(BF16)
        wwt = col(wl, 7).T.astype(BF16)

        x2, h2 = _ffn(x2, ffn1_norm_pre[l], ffn1_norm_post[l],
                      ffn1_w_in[l].astype(BF16), ffn1_w_out[l].astype(BF16), g_next=mix_norm_pre[l])

        u, vln, k, gate, kidx = _proj_nat(h2, w_nat, w_kidx, sgu_ln_g[l], sgu_ln_b[l])
        qt, vt, qit, wit = _proj_t(h2.reshape(b, s, d), wt_cat, wwt)
        yb = _attention(kidx.reshape(b, s, IDX_DIM), qit, wit, k.reshape(b, s, att_w), vt, qt, bias, top_k)
        x2 = _merge(x2, u, vln, yb.reshape(b * s, att_w), gate,
                    sgu_w_s[l], sgu_b[l].reshape(SGU_GROUPS, SGU_CHUNK, 1),
                    w_branch_a[l].astype(BF16), w_branch_b[l].astype(BF16), w_out[l].astype(BF16),
                    mix_norm_post[l])

        (x2,) = _ffn(x2, ffn2_norm_pre[l], ffn2_norm_post[l],
                     ffn2_w_in[l].astype(BF16), ffn2_w_out[l].astype(BF16))
    return x2.reshape(b, s, d)
```

```python
import functools
import math

import jax
import jax.numpy as jnp
from jax import lax
from jax.experimental import pallas as pl
from jax.experimental.pallas import tpu as pltpu

F32 = jnp.float32
BF16 = jnp.bfloat16
I32 = jnp.int32
I16 = jnp.int16

SGU_GROUPS = 8
SGU_CHUNK = 128
ATT_HEADS = 8
ATT_HEAD_DIM = 128
IDX_HEADS = 16
IDX_DIM = 64
TOPK_MAX = 256
NUM_BUCKETS = 32
MAX_DISTANCE = 128
NORM_EPS = 1e-6
LN_EPS = 1e-5

V7X_VMEM_BYTES = 64 * 1024 * 1024
VMEM_LIMIT = V7X_VMEM_BYTES - 8 * 1024 * 1024
SUBLANES = 8
BF16_TILE_ROWS = 16

FFN_TM = 512
FFN_TF = 512
PROJ_TM = 512
PROJ_BN = 1024
PROJ_CH = 256
MM_AHEAD = 1
MERGE_TM = 256
ATT_TQ = 256
ATT_KU = 256
ACC_ROWS = ATT_HEAD_DIM + BF16_TILE_ROWS
QK_AHEAD = ATT_HEADS

INT_MIN = -(2 ** 31)
I16_MIN, I16_MAX, I16_BIAS = -(2 ** 15), 2 ** 15 - 1, 2 ** 15
MASKED_LOGIT = -1e30
LOG2E = math.log2(math.e)


def _rms(xf, g):
    ms = jnp.mean(xf * xf, axis=-1, keepdims=True)
    return xf * lax.rsqrt(ms + NORM_EPS) * g


def _gelu_tanh(x):
    c = math.sqrt(2.0 / math.pi)
    return x * (0.5 * (1.0 + jnp.tanh(c * (x + 0.044715 * (x * x * x)))))


def _dot(a, b):
    return jnp.dot(a, b, preferred_element_type=F32)


def _dot_nt(a, b):
    return lax.dot_general(a, b, (((1,), (1,)), ((), ())), preferred_element_type=F32)


def _ffn_body(*refs, emit_next):
    if emit_next:
        x_ref, gpre_ref, gpost_ref, gnext_ref, wa_ref, wb_ref, wo_ref, o_ref, hn_ref, h_scr, acc_scr = refs
    else:
        x_ref, gpre_ref, gpost_ref, wa_ref, wb_ref, wo_ref, o_ref, h_scr, acc_scr = refs
    j = pl.program_id(1)

    @pl.when(j == 0)
    def _():
        h_scr[...] = _rms(x_ref[...], gpre_ref[...]).astype(BF16)
        acc_scr[...] = jnp.zeros_like(acc_scr)

    h = h_scr[...]
    a = _dot(h, wa_ref[...])
    b = _dot(h, wb_ref[...])
    g = (a * jax.nn.sigmoid(a) * b).astype(BF16)
    acc_scr[...] += _dot(g, wo_ref[...])

    @pl.when(j == pl.num_programs(1) - 1)
    def _():
        y = x_ref[...] + 0.5 * _rms(acc_scr[...], gpost_ref[...])
        o_ref[...] = y
        if emit_next:
            hn_ref[...] = _rms(y, gnext_ref[...]).astype(BF16)


def _ffn(x2, g_pre, g_post, w_in, w_out, layer, g_next=None):
    n, d = x2.shape
    d_ff = w_out.shape[1]
    nf = d_ff // FFN_TF
    emit_next = g_next is not None
    vec = pl.BlockSpec((1, d), lambda i, j: (0, 0))
    tok = pl.BlockSpec((FFN_TM, d), lambda i, j: (i, 0))
    gains = [g_pre, g_post] + ([g_next] if emit_next else [])
    out_shape = [jax.ShapeDtypeStruct((n, d), F32)] + ([jax.ShapeDtypeStruct((n, d), BF16)] if emit_next else [])
    return pl.pallas_call(
        functools.partial(_ffn_body, emit_next=emit_next),
        grid=(n // FFN_TM, nf),
        in_specs=[tok] + [vec] * len(gains) + [
            pl.BlockSpec((None, d, FFN_TF), lambda i, j: (layer, 0, j)),
            pl.BlockSpec((None, d, FFN_TF), lambda i, j: (layer, 0, j + nf)),
            pl.BlockSpec((None, FFN_TF, d), lambda i, j: (layer, j, 0)),
        ],
        out_specs=[tok] * len(out_shape),
        out_shape=out_shape,
        scratch_shapes=[pltpu.VMEM((FFN_TM, d), BF16), pltpu.VMEM((FFN_TM, d), F32)],
        compiler_params=pltpu.CompilerParams(
            dimension_semantics=("parallel", "arbitrary"), vmem_limit_bytes=VMEM_LIMIT),
        name="ffn",
    )(x2, *[g.reshape(1, d) for g in gains], w_in, w_in, w_out)


def _chunked(n_chunks, matmul, epilogue):
    z = [matmul(c) for c in range(min(MM_AHEAD, n_chunks))]
    for c in range(n_chunks):
        if c + MM_AHEAD < n_chunks:
            z.append(matmul(c + MM_AHEAD))
        epilogue(c, z[c])
        z[c] = None


def _proj_nat_body(h_ref, w_ref, wki_ref, wg_ref, lng_ref, lnb_ref,
                   u_ref, vln_ref, k_ref, gate_ref, kidx_ref, v_scr):
    j = pl.program_id(1)
    ch = PROJ_CH
    nc = PROJ_BN // ch
    cols = lambda c: slice(c * ch, (c + 1) * ch)
    matmul = lambda c: _dot(h_ref[...], w_ref[:, cols(c)])
    matmul_gate = lambda c: _dot(h_ref[...], wg_ref[:, cols(c)])

    @pl.when(j == 0)
    def _():
        kidx_ref[...] = _dot(h_ref[...], wki_ref[...]).astype(BF16)

        def store_u(c, z):
            u_ref[:, cols(c)] = _gelu_tanh(z).astype(BF16)
        _chunked(nc, matmul, store_u)

    @pl.when(j == 1)
    def _():
        def store_v(c, z):
            v_scr[:, cols(c)] = _gelu_tanh(z)
        _chunked(nc, matmul, store_v)
        v = v_scr[...]
        mu = jnp.mean(v, axis=-1, keepdims=True)
        vc = v - mu
        var = jnp.mean(vc * vc, axis=-1, keepdims=True)
        vln_ref[...] = (vc * lax.rsqrt(var + LN_EPS) * lng_ref[...] + lnb_ref[...]).astype(BF16)

    @pl.when(j == 2)
    def _():
        def store_k(c, z):
            k_ref[:, cols(c)] = z.astype(BF16)
        _chunked(nc, matmul, store_k)

    @pl.when(j >= 3)
    def _():
        def store_gate(c, z):
            gate_ref[:, cols(c)] = jax.nn.sigmoid(z).astype(BF16)
        _chunked(nc, matmul_gate, store_gate)


def _proj_nat(h2, w_in, w_kidx, w_gate, ln_g, ln_b, layer, blocks):
    n, d = h2.shape
    width = PROJ_BN
    ngate = w_gate.shape[2]
    n_in = len(blocks)
    assert n_in == 3
    nj = n_in + ngate // width
    tok = lambda i, j: (i, 0)

    def in_block(i, j):
        blk = blocks[n_in - 1]
        for t in range(n_in - 2, -1, -1):
            blk = jnp.where(j == t, blocks[t], blk)
        return (layer, 0, blk)

    return pl.pallas_call(
        _proj_nat_body,
        grid=(n // PROJ_TM, nj),
        in_specs=[
            pl.BlockSpec((PROJ_TM, d), tok),
            pl.BlockSpec((None, d, width), in_block),
            pl.BlockSpec((None, d, IDX_DIM), lambda i, j: (layer, 0, 0)),
            pl.BlockSpec((None, d, width), lambda i, j: (layer, 0, jnp.maximum(j - n_in, 0))),
            pl.BlockSpec((1, width), lambda i, j: (0, 0)),
            pl.BlockSpec((1, width), lambda i, j: (0, 0)),
        ],
        out_specs=[
            pl.BlockSpec((PROJ_TM, width), tok),
            pl.BlockSpec((PROJ_TM, width), tok),
            pl.BlockSpec((PROJ_TM, width), tok),
            pl.BlockSpec((PROJ_TM, width), lambda i, j: (i, jnp.maximum(j - 3, 0))),
            pl.BlockSpec((PROJ_TM, IDX_DIM), tok),
        ],
        out_shape=[
            jax.ShapeDtypeStruct((n, width), BF16),
            jax.ShapeDtypeStruct((n, width), BF16),
            jax.ShapeDtypeStruct((n, width), BF16),
            jax.ShapeDtypeStruct((n, ngate), BF16),
            jax.ShapeDtypeStruct((n, IDX_DIM), BF16),
        ],
        scratch_shapes=[pltpu.VMEM((PROJ_TM, width), F32)],
        compiler_params=pltpu.CompilerParams(
            dimension_semantics=("parallel", "arbitrary"), vmem_limit_bytes=VMEM_LIMIT),
        name="proj_nat",
    )(h2, w_in, w_kidx, w_gate, ln_g.reshape(1, width), ln_b.reshape(1, width))


def _proj_t_body(h_ref, wt_ref, wwt_ref, qt_ref, vt_ref, qit_ref, wit_ref):
    j = pl.program_id(2)
    ch = PROJ_CH
    nc = PROJ_BN // ch
    rows = lambda c: slice(c * ch, (c + 1) * ch)
    matmul = lambda c: _dot_nt(wt_ref[rows(c), :], h_ref[0])

    @pl.when(j == 0)
    def _():
        wit_ref[0] = _dot_nt(wwt_ref[...], h_ref[0])

        def store_q(c, zt):
            qt_ref[0, rows(c), :] = (zt * (ATT_HEAD_DIM ** -0.5 * LOG2E)).astype(BF16)
        _chunked(nc, matmul, store_q)

    @pl.when(j == 1)
    def _():
        def store_v(c, zt):
            for cc in range(PROJ_TM // ATT_KU):
                vt_ref[0, cc, rows(c), :] = zt[:, cc * ATT_KU:(cc + 1) * ATT_KU].astype(BF16)
        _chunked(nc, matmul, store_v)

    @pl.when(j == 2)
    def _():
        def store_qi(c, zt):
            qit_ref[0, rows(c), :] = zt.astype(BF16)
        _chunked(nc, matmul, store_qi)


def _proj_t(h3, wt_cat, wwt, layer):
    b, s, d = h3.shape
    width = PROJ_BN
    nch = PROJ_TM // ATT_KU
    feat = lambda bi, si, j: (bi, 0, si)
    return pl.pallas_call(
        _proj_t_body,
        grid=(b, s // PROJ_TM, 3),
        in_specs=[
            pl.BlockSpec((1, PROJ_TM, d), lambda bi, si, j: (bi, si, 0)),
            pl.BlockSpec((None, width, d), lambda bi, si, j: (layer, j, 0)),
            pl.BlockSpec((None, IDX_HEADS, d), lambda bi, si, j: (layer, 0, 0)),
        ],
        out_specs=[
            pl.BlockSpec((1, width, PROJ_TM), feat),
            pl.BlockSpec((1, nch, width, ATT_KU), lambda bi, si, j: (bi, si, 0, 0)),
            pl.BlockSpec((1, width, PROJ_TM), feat),
            pl.BlockSpec((1, IDX_HEADS, PROJ_TM), feat),
        ],
        out_shape=[
            jax.ShapeDtypeStruct((b, width, s), BF16),
            jax.ShapeDtypeStruct((b, s // ATT_KU, width, ATT_KU), BF16),
            jax.ShapeDtypeStruct((b, width, s), BF16),
            jax.ShapeDtypeStruct((b, IDX_HEADS, s), F32),
        ],
        compiler_params=pltpu.CompilerParams(
            dimension_semantics=("parallel", "parallel", "arbitrary"), vmem_limit_bytes=VMEM_LIMIT),
        name="proj_t",
    )(h3, wt_cat, wwt)


N_NEAR_UNITS = 2


def _bias_body(rb_ref, o_ref):
    h = pl.program_id(0)
    r = lax.broadcasted_iota(I32, (ATT_KU, ATT_TQ), 0)
    c = lax.broadcasted_iota(I32, (ATT_KU, ATT_TQ), 1)
    max_exact = NUM_BUCKETS // 2
    far = rb_ref[NUM_BUCKETS - 1, h]
    for t in range(N_NEAR_UNITS):
        dist = jnp.maximum(t * ATT_KU + c - r, 0)
        nf = jnp.maximum(dist, 1).astype(F32)
        large = max_exact + (jnp.log(nf / max_exact) / math.log(MAX_DISTANCE / max_exact)
                             * (NUM_BUCKETS - max_exact)).astype(I32)
        large = jnp.minimum(large, NUM_BUCKETS - 1)
        bucket = jnp.where(dist < max_exact, dist, large)
        val = jnp.zeros((ATT_KU, ATT_TQ), F32)
        for bkt in range(NUM_BUCKETS):
            val = jnp.where(bucket == bkt, rb_ref[bkt, h], val)
        o_ref[0, t] = (val - far) * LOG2E


def _bias_tiles(rel_bias):
    assert N_NEAR_UNITS * ATT_KU - (ATT_KU - 1) >= MAX_DISTANCE
    return pl.pallas_call(
        _bias_body,
        grid=(ATT_HEADS,),
        in_specs=[pl.BlockSpec(memory_space=pltpu.SMEM)],
        out_specs=pl.BlockSpec((1, N_NEAR_UNITS, ATT_KU, ATT_TQ), lambda h: (h, 0, 0, 0)),
        out_shape=jax.ShapeDtypeStruct((ATT_HEADS, N_NEAR_UNITS, ATT_KU, ATT_TQ), F32),
        name="bias_tiles",
    )(rel_bias)


def _tree_sum(xs):
    while len(xs) > 1:
        xs = [a + b for a, b in zip(xs[::2], xs[1::2])] + ([xs[-1]] if len(xs) % 2 else [])
    return xs[0]


def _sublane_allmax(x):
    for shift in (4, 2, 1):
        x = jnp.maximum(x, pltpu.roll(x, shift, axis=0))
    return x


def _attn_body(kidx_ref, qit_ref, wit_ref, k_ref, vt_ref, qt_ref, bias_ref, o_ref,
               key_scr, hi_scr, lo_scr, m_scr, acc_scr, *, top_k):
    i = pl.program_id(1)
    ku, tq, sl, pk = ATT_KU, ATT_TQ, SUBLANES, BF16_TILE_ROWS
    n_units = i + 1
    q0 = i * tq

    w_all = wit_ref[0] * (IDX_HEADS ** -0.5 * IDX_DIM ** -0.5)
    row = lax.broadcasted_iota(I32, (ku, tq), 0)
    col = lax.broadcasted_iota(I32, (ku, tq), 1)

    def score_unit(u, carry):
        r0 = pl.multiple_of(u * ku, ku)
        kch = kidx_ref[0, pl.ds(r0, ku), :]
        acc = jnp.zeros((ku, tq), F32)
        for h in range(IDX_HEADS):
            d = _dot(kch, qit_ref[0, h * IDX_DIM:(h + 1) * IDX_DIM, :])
            acc = acc + jnp.maximum(d, 0.0) * w_all[h:h + 1, :]
        bits = lax.bitcast_convert_type(acc, I32)
        key = jnp.where(bits < 0, bits ^ jnp.int32(0x7FFFFFFF), bits)
        key = jnp.where(r0 + row <= q0 + col, key, INT_MIN)
        key_scr[pl.ds(r0, ku), :] = key
        hi_scr[pl.ds(r0, ku), :] = lax.shift_right_arithmetic(key, 16).astype(I16)
        lo_scr[pl.ds(r0, ku), :] = ((key & 0xFFFF) - I16_BIAS).astype(I16)
        return carry

    lax.fori_loop(0, n_units, score_unit, 0)

    one_b, zero_b = jnp.ones((), BF16), jnp.zeros((), BF16)

    def count_ge(ref, cand):
        cand16 = jnp.broadcast_to(cand, (pk, tq)).astype(I16)

        def body(u, cnt):
            r0 = pl.multiple_of(u * ku, ku)
            blk = ref[pl.ds(r0, ku), :].reshape(ku // pk, pk, tq)
            hit = jnp.where(blk >= cand16[None], one_b, zero_b)
            return cnt + _tree_sum([hit[g] for g in range(ku // pk)]).astype(F32)

        cnt = lax.fori_loop(0, n_units, body, jnp.zeros((pk, tq), F32))
        return jnp.sum(cnt, axis=0, keepdims=True)

    def kth_largest(ref, kvec):
        zero = jnp.zeros((1, tq), I32)
        prefix = jnp.where(count_ge(ref, zero) >= kvec, zero, I16_MIN)

        def descend(t, prefix):
            cand = prefix | jnp.left_shift(jnp.int32(1), 14 - t)
            return jnp.where(count_ge(ref, cand) >= kvec, cand, prefix)

        return lax.fori_loop(0, 15, descend, prefix)

    def count_gt(ref, v):
        return jnp.where(v == I16_MAX, 0.0, count_ge(ref, jnp.minimum(v + 1, I16_MAX)))

    k_f = jnp.full((1, tq), top_k, F32)
    thr_hi = kth_largest(hi_scr, k_f)
    k_low = k_f - count_gt(hi_scr, thr_hi)
    thr_hi16 = jnp.broadcast_to(thr_hi, (pk, tq)).astype(I16)

    def keep_low_of_winners(u, carry):
        r0 = pl.multiple_of(u * ku, ku)
        hi = hi_scr[pl.ds(r0, ku), :].reshape(ku // pk, pk, tq)
        lo = lo_scr[pl.ds(r0, ku), :].reshape(ku // pk, pk, tq)
        lo_scr[pl.ds(r0, ku), :] = jnp.where(hi == thr_hi16[None], lo, jnp.int16(I16_MIN)).reshape(ku, tq)
        return carry

    lax.fori_loop(0, n_units, keep_low_of_winners, 0)
    thr_lo = kth_largest(lo_scr, k_low)
    thr_raw = thr_hi * (1 << 16) + (thr_lo + I16_BIAS)
    thr = jnp.maximum(thr_raw, INT_MIN + 1)

    n_greater = count_gt(lo_scr, thr_lo)
    n_tied_ok = k_low - n_greater
    n_tied = count_ge(lo_scr, thr_lo) - n_greater
    has_excess = jnp.logical_and(n_tied > n_tied_ok, thr_raw > INT_MIN)

    @pl.when(jnp.max(jnp.where(has_excess, 1.0, 0.0)) > 0.0)
    def _():
        earlier = (lax.broadcasted_iota(I32, (ku, ku), 0) > lax.broadcasted_iota(I32, (ku, ku), 1))
        earlier = jnp.where(earlier, 1.0, 0.0).astype(BF16)

        def demote(u, seen):
            r0 = pl.multiple_of(u * ku, ku)
            key = key_scr[pl.ds(r0, ku), :]
            tied = key == thr
            rank = _dot(earlier, jnp.where(tied, 1.0, 0.0).astype(BF16)) + seen
            key_scr[pl.ds(r0, ku), :] = jnp.where(jnp.logical_and(tied, rank >= n_tied_ok), INT_MIN, key)
            return seen + jnp.sum(jnp.where(tied, 1.0, 0.0), axis=0, keepdims=True)

        lax.fori_loop(0, n_units, demote, jnp.zeros((1, tq), F32))

    m_scr[...] = jnp.full(m_scr.shape, MASKED_LOGIT, F32)
    acc_scr[...] = jnp.zeros(acc_scr.shape, F32)
    ones_rows = jnp.ones((ACC_ROWS - ATT_HEAD_DIM, ku), BF16)

    def unit(u, near_tile):
        r0 = pl.multiple_of(u * ku, ku)
        keep = key_scr[pl.ds(r0, ku), :].reshape(ku // sl, sl, tq) >= thr[None]
        mask_add = jnp.where(keep, 0.0, MASKED_LOGIT)

        def qk(h):
            hs = slice(h * ATT_HEAD_DIM, (h + 1) * ATT_HEAD_DIM)
            s = _dot(k_ref[0, pl.ds(r0, ku), hs], qt_ref[0, hs, :])
            if near_tile is not None:
                s = s + bias_ref[h, near_tile]
            return s

        pending = [qk(h) for h in range(min(QK_AHEAD, ATT_HEADS))]
        for h in range(ATT_HEADS):
            hs = slice(h * ATT_HEAD_DIM, (h + 1) * ATT_HEAD_DIM)
            s = pending.pop(0)
            if h + QK_AHEAD < ATT_HEADS:
                pending.append(qk(h + QK_AHEAD))
            s = s.reshape(ku // sl, sl, tq) + mask_add
            m_old = m_scr[h]
            m_new = jnp.maximum(m_old, _sublane_allmax(jnp.max(s, axis=0)))
            alpha = jnp.exp2(m_old - m_new)
            p = jnp.exp2(s - m_new[None])
            m_scr[h] = m_new
            v_ext = jnp.concatenate([vt_ref[0, u, hs, :], ones_rows], axis=0)
            pv = _dot(v_ext, p.reshape(ku, tq).astype(BF16))
            acc = acc_scr[h].reshape(ACC_ROWS // sl, sl, tq) * alpha[None]
            acc_scr[h] = pv + acc.reshape(ACC_ROWS, tq)

    lax.fori_loop(0, n_units - N_NEAR_UNITS, lambda u, c: (unit(u, None), c)[1], 0)
    for t in range(N_NEAR_UNITS - 1, -1, -1):
        lax.fori_loop(0, jnp.minimum(n_units - t, 1), lambda _, c, t=t: (unit(i - t, t), c)[1], 0)

    for h in range(ATT_HEADS):
        hs = slice(h * ATT_HEAD_DIM, (h + 1) * ATT_HEAD_DIM)
        l = acc_scr[h, ATT_HEAD_DIM:ATT_HEAD_DIM + 1, :]
        o_ref[0, :, hs] = (acc_scr[h, :ATT_HEAD_DIM, :] / l).T.astype(BF16)


def _attention(kidx, qit, wit, k, vt, qt, bias, top_k):
    b, s, width = k.shape
    one = pl.Buffered(1)
    return pl.pallas_call(
        functools.partial(_attn_body, top_k=top_k),
        grid=(b, s // ATT_TQ),
        in_specs=[
            pl.BlockSpec((1, s, IDX_DIM), lambda bi, i: (bi, 0, 0), pipeline_mode=one),
            pl.BlockSpec((1, IDX_HEADS * IDX_DIM, ATT_TQ), lambda bi, i: (bi, 0, i)),
            pl.BlockSpec((1, IDX_HEADS, ATT_TQ), lambda bi, i: (bi, 0, i)),
            pl.BlockSpec((1, s, width), lambda bi, i: (bi, 0, 0), pipeline_mode=one),
            pl.BlockSpec((1, s // ATT_KU, width, ATT_KU), lambda bi, i: (bi, 0, 0, 0), pipeline_mode=one),
            pl.BlockSpec((1, width, ATT_TQ), lambda bi, i: (bi, 0, i)),
            pl.BlockSpec(bias.shape, lambda bi, i: (0, 0, 0, 0), pipeline_mode=one),
        ],
        out_specs=pl.BlockSpec((1, ATT_TQ, width), lambda bi, i: (bi, i, 0)),
        out_shape=jax.ShapeDtypeStruct((b, s, width), BF16),
        scratch_shapes=[
            pltpu.VMEM((s, ATT_TQ), I32),
            pltpu.VMEM((s, ATT_TQ), I16),
            pltpu.VMEM((s, ATT_TQ), I16),
            pltpu.VMEM((ATT_HEADS, SUBLANES, ATT_TQ), F32),
            pltpu.VMEM((ATT_HEADS, ACC_ROWS, ATT_TQ), F32),
        ],
        compiler_params=pltpu.CompilerParams(
            dimension_semantics=("parallel", "arbitrary"), vmem_limit_bytes=VMEM_LIMIT),
        name="dsa_attention",
    )(kidx, qit, wit, k, vt, qt, bias)


def _merge_body(x_ref, u_ref, vln_ref, yb_ref, gate_ref, ws_ref, bs_ref, wa_ref, wb_ref, wo_ref,
                gpost_ref, o_ref, ya_scr):
    ch = SGU_CHUNK
    d = x_ref.shape[1]
    tril = (lax.broadcasted_iota(I32, (ch, ch), 0) >= lax.broadcasted_iota(I32, (ch, ch), 1))
    for g in range(SGU_GROUPS):
        gs = slice(g * ch, (g + 1) * ch)
        wsg = jnp.where(tril, ws_ref[g], 0.0).astype(BF16)
        for c in range(MERGE_TM // ch):
            cs = slice(c * ch, (c + 1) * ch)
            mixed = _dot(wsg, vln_ref[cs, gs]) + bs_ref[g]
            ya_scr[cs, gs] = (u_ref[cs, gs].astype(F32) * mixed).astype(BF16)
    ma = _dot(ya_scr[...], wa_ref[...])
    mb = _dot(yb_ref[...], wb_ref[...])
    merged = gate_ref[:, :d].astype(F32) * ma + gate_ref[:, d:].astype(F32) * mb
    o = _dot(merged.astype(BF16), wo_ref[...])
    o_ref[...] = x_ref[...] + _rms(o, gpost_ref[...])


def _merge(x2, u, vln, yb, gate, w_s, b_s, w_a, w_b, w_o, g_post, layer):
    n, d = x2.shape
    width = u.shape[1]
    one = pl.Buffered(1)
    tok = lambda i: (i, 0)
    const2 = lambda i: (0, 0)
    whole = lambda a: pl.BlockSpec((None,) + a.shape[1:], lambda i: (layer,) + (0,) * (a.ndim - 1),
                                   pipeline_mode=one)
    return pl.pallas_call(
        _merge_body,
        grid=(n // MERGE_TM,),
        in_specs=[
            pl.BlockSpec((MERGE_TM, d), tok),
            pl.BlockSpec((MERGE_TM, width), tok),
            pl.BlockSpec((MERGE_TM, width), tok),
            pl.BlockSpec((MERGE_TM, width), tok),
            pl.BlockSpec((MERGE_TM, 2 * d), tok),
            whole(w_s), whole(b_s), whole(w_a), whole(w_b), whole(w_o),
            pl.BlockSpec((1, d), const2),
        ],
        out_specs=pl.BlockSpec((MERGE_TM, d), tok),
        out_shape=jax.ShapeDtypeStruct((n, d), F32),
        scratch_shapes=[pltpu.VMEM((MERGE_TM, width), BF16)],
        compiler_params=pltpu.CompilerParams(
            dimension_semantics=("parallel",), vmem_limit_bytes=VMEM_LIMIT),
        name="merge",
    )(x2, u, vln, yb, gate, w_s, b_s, w_a, w_b, w_o, g_post.reshape(1, d))


def kernel(x, ffn1_norm_pre, ffn1_norm_post, ffn1_w_in, ffn1_w_out, mix_norm_pre, mix_norm_post, w_in,
           sgu_ln_g, sgu_ln_b, sgu_w_s, sgu_b, rel_bias, w_branch_a, w_branch_b, w_gate, w_out,
           ffn2_norm_pre, ffn2_norm_post, ffn2_w_in, ffn2_w_out):
    b, s, d = x.shape
    depth = w_in.shape[0]
    sgu_w = sgu_ln_g.shape[1]
    att_w = ATT_HEADS * ATT_HEAD_DIM
    idx_w = IDX_HEADS * IDX_DIM
    top_k = min(TOPK_MAX, s // 4)
    assert s % PROJ_TM == 0 and s % ATT_TQ == 0 and (b * s) % FFN_TM == 0
    assert sgu_w == PROJ_BN and att_w == PROJ_BN and idx_w == PROJ_BN and ATT_KU == ATT_TQ

    sizes = (sgu_w, sgu_w, att_w, att_w, att_w, idx_w, IDX_DIM, IDX_HEADS)
    offs = [0]
    for sz in sizes:
        offs.append(offs[-1] + sz)
    col = lambda w, idx: w[:, :, offs[idx]:offs[idx + 1]]
    assert all(offs[idx] % PROJ_BN == 0 for idx in (0, 1, 3))
    nat_blocks = tuple(offs[idx] // PROJ_BN for idx in (0, 1, 3))

    bf = lambda w: w.astype(BF16)
    w_in_b, w_gate_b = bf(w_in), bf(w_gate)
    wt_cat = jnp.swapaxes(jnp.concatenate([col(w_in_b, 2), col(w_in_b, 4), col(w_in_b, 5)], axis=2), 1, 2)
    wwt = jnp.swapaxes(col(w_in_b, 7), 1, 2)
    w_kidx = col(w_in_b, 6)
    ffn1_in_b, ffn1_out_b, ffn2_in_b, ffn2_out_b = bf(ffn1_w_in), bf(ffn1_w_out), bf(ffn2_w_in), bf(ffn2_w_out)
    w_a_b, w_b_b, w_o_b = bf(w_branch_a), bf(w_branch_b), bf(w_out)
    b_s = sgu_b.reshape(depth, SGU_GROUPS, SGU_CHUNK, 1)

    bias = _bias_tiles(rel_bias)
    x2 = x.reshape(b * s, d)
    for l in range(depth):
        x2, h2 = _ffn(x2, ffn1_norm_pre[l], ffn1_norm_post[l], ffn1_in_b, ffn1_out_b, l,
                      g_next=mix_norm_pre[l])
        u, vln, k, gate, kidx = _proj_nat(h2, w_in_b, w_kidx, w_gate_b, sgu_ln_g[l], sgu_ln_b[l], l,
                                          nat_blocks)
        qt, vt, qit, wit = _proj_t(h2.reshape(b, s, d), wt_cat, wwt, l)
        yb = _attention(kidx.reshape(b, s, IDX_DIM), qit, wit, k.reshape(b, s, att_w), vt, qt, bias, top_k)
        x2 = _merge(x2, u, vln, yb.reshape(b * s, att_w), gate,
                    sgu_w_s, b_s, w_a_b, w_b_b, w_o_b, mix_norm_post[l], l)
        (x2,) = _ffn(x2, ffn2_norm_pre[l], ffn2_norm_post[l], ffn2_in_b, ffn2_out_b, l)
    return x2.reshape(b, s, d)
```

```python
import functools
import math

import jax
import jax.numpy as jnp
from jax import lax
from jax.experimental import pallas as pl
from jax.experimental.pallas import tpu as pltpu

F32 = jnp.float32
BF16 = jnp.bfloat16
I32 = jnp.int32
I16 = jnp.int16

SGU_GROUPS = 8
SGU_CHUNK = 128
ATT_HEADS = 8
ATT_HEAD_DIM = 128
IDX_HEADS = 16
IDX_DIM = 64
TOPK_MAX = 256
NUM_BUCKETS = 32
MAX_DISTANCE = 128
NORM_EPS = 1e-6
LN_EPS = 1e-5

V7X_VMEM_BYTES = 64 * 1024 * 1024
VMEM_LIMIT = V7X_VMEM_BYTES - 8 * 1024 * 1024
SUBLANES = 8
BF16_TILE_ROWS = 16

FFN_TM = 512
FFN_TF = 512
PROJ_TM = 1024
PROJ_BN = 1024
PROJ_CH = 256
MM_AHEAD = 1
MERGE_TM = 256
ATT_TQ = 256
ATT_KU = 256
ACC_ROWS = ATT_HEAD_DIM + BF16_TILE_ROWS
QK_AHEAD = ATT_HEADS

INT_MIN = -(2 ** 31)
I16_MIN, I16_MAX, I16_BIAS = -(2 ** 15), 2 ** 15 - 1, 2 ** 15
MASKED_LOGIT = -1e30
LOG2E = math.log2(math.e)


def _rms(xf, g):
    ms = jnp.mean(xf * xf, axis=-1, keepdims=True)
    return xf * lax.rsqrt(ms + NORM_EPS) * g


def _gelu_tanh(x):
    c = math.sqrt(2.0 / math.pi)
    return x * (0.5 * (1.0 + jnp.tanh(c * (x + 0.044715 * (x * x * x)))))


def _dot(a, b):
    return jnp.dot(a, b, preferred_element_type=F32)


def _dot_nt(a, b):
    return lax.dot_general(a, b, (((1,), (1,)), ((), ())), preferred_element_type=F32)


def _ffn_body(*refs, emit_next):
    if emit_next:
        x_ref, gpre_ref, gpost_ref, gnext_ref, wa_ref, wb_ref, wo_ref, o_ref, hn_ref, h_scr, acc_scr = refs
    else:
        x_ref, gpre_ref, gpost_ref, wa_ref, wb_ref, wo_ref, o_ref, h_scr, acc_scr = refs
    j = pl.program_id(1)

    @pl.when(j == 0)
    def _():
        h_scr[...] = _rms(x_ref[...], gpre_ref[...]).astype(BF16)
        acc_scr[...] = jnp.zeros_like(acc_scr)

    h = h_scr[...]
    a = _dot(h, wa_ref[...])
    b = _dot(h, wb_ref[...])
    g = (a * jax.nn.sigmoid(a) * b).astype(BF16)
    acc_scr[...] += _dot(g, wo_ref[...])

    @pl.when(j == pl.num_programs(1) - 1)
    def _():
        y = x_ref[...] + 0.5 * _rms(acc_scr[...], gpost_ref[...])
        o_ref[...] = y
        if emit_next:
            hn_ref[...] = _rms(y, gnext_ref[...]).astype(BF16)


def _ffn(x2, g_pre, g_post, w_in, w_out, layer, g_next=None):
    n, d = x2.shape
    d_ff = w_out.shape[1]
    nf = d_ff // FFN_TF
    emit_next = g_next is not None
    vec = pl.BlockSpec((1, d), lambda i, j: (0, 0))
    tok = pl.BlockSpec((FFN_TM, d), lambda i, j: (i, 0))
    gains = [g_pre, g_post] + ([g_next] if emit_next else [])
    out_shape = [jax.ShapeDtypeStruct((n, d), F32)] + ([jax.ShapeDtypeStruct((n, d), BF16)] if emit_next else [])
    return pl.pallas_call(
        functools.partial(_ffn_body, emit_next=emit_next),
        grid=(n // FFN_TM, nf),
        in_specs=[tok] + [vec] * len(gains) + [
            pl.BlockSpec((None, d, FFN_TF), lambda i, j: (layer, 0, j)),
            pl.BlockSpec((None, d, FFN_TF), lambda i, j: (layer, 0, j + nf)),
            pl.BlockSpec((None, FFN_TF, d), lambda i, j: (layer, j, 0)),
        ],
        out_specs=[tok] * len(out_shape),
        out_shape=out_shape,
        scratch_shapes=[pltpu.VMEM((FFN_TM, d), BF16), pltpu.VMEM((FFN_TM, d), F32)],
        compiler_params=pltpu.CompilerParams(
            dimension_semantics=("parallel", "arbitrary"), vmem_limit_bytes=VMEM_LIMIT),
        name="ffn",
    )(x2, *[g.reshape(1, d) for g in gains], w_in, w_in, w_out)


def _chunked(n_chunks, matmul, epilogue):
    z = [matmul(c) for c in range(min(MM_AHEAD, n_chunks))]
    for c in range(n_chunks):
        if c + MM_AHEAD < n_chunks:
            z.append(matmul(c + MM_AHEAD))
        epilogue(c, z[c])
        z[c] = None


def _proj_nat_body(h_ref, w_ref, wki_ref, wg_ref, lng_ref, lnb_ref,
                   u_ref, vln_ref, k_ref, gate_ref, kidx_ref, v_scr):
    j = pl.program_id(1)
    ch = PROJ_CH
    nc = PROJ_BN // ch
    cols = lambda c: slice(c * ch, (c + 1) * ch)
    matmul = lambda c: _dot(h_ref[...], w_ref[:, cols(c)])
    matmul_gate = lambda c: _dot(h_ref[...], wg_ref[:, cols(c)])

    @pl.when(j == 0)
    def _():
        kidx_ref[...] = _dot(h_ref[...], wki_ref[...]).astype(BF16)

        def store_u(c, z):
            u_ref[:, cols(c)] = _gelu_tanh(z).astype(BF16)
        _chunked(nc, matmul, store_u)

    @pl.when(j == 1)
    def _():
        def store_v(c, z):
            v_scr[:, cols(c)] = _gelu_tanh(z)
        _chunked(nc, matmul, store_v)
        v = v_scr[...]
        mu = jnp.mean(v, axis=-1, keepdims=True)
        vc = v - mu
        var = jnp.mean(vc * vc, axis=-1, keepdims=True)
        vln_ref[...] = (vc * lax.rsqrt(var + LN_EPS) * lng_ref[...] + lnb_ref[...]).astype(BF16)

    @pl.when(j == 2)
    def _():
        def store_k(c, z):
            k_ref[:, cols(c)] = z.astype(BF16)
        _chunked(nc, matmul, store_k)

    @pl.when(j >= 3)
    def _():
        def store_gate(c, z):
            gate_ref[:, cols(c)] = jax.nn.sigmoid(z).astype(BF16)
        _chunked(nc, matmul_gate, store_gate)


def _proj_nat(h2, w_in, w_kidx, w_gate, ln_g, ln_b, layer, blocks):
    n, d = h2.shape
    width = PROJ_BN
    ngate = w_gate.shape[2]
    n_in = len(blocks)
    assert n_in == 3
    nj = n_in + ngate // width
    tok = lambda i, j: (i, 0)

    def in_block(i, j):
        blk = blocks[n_in - 1]
        for t in range(n_in - 2, -1, -1):
            blk = jnp.where(j == t, blocks[t], blk)
        return (layer, 0, blk)

    return pl.pallas_call(
        _proj_nat_body,
        grid=(n // PROJ_TM, nj),
        in_specs=[
            pl.BlockSpec((PROJ_TM, d), tok),
            pl.BlockSpec((None, d, width), in_block),
            pl.BlockSpec((None, d, IDX_DIM), lambda i, j: (layer, 0, 0)),
            pl.BlockSpec((None, d, width), lambda i, j: (layer, 0, jnp.maximum(j - n_in, 0))),
            pl.BlockSpec((1, width), lambda i, j: (0, 0)),
            pl.BlockSpec((1, width), lambda i, j: (0, 0)),
        ],
        out_specs=[
            pl.BlockSpec((PROJ_TM, width), tok),
            pl.BlockSpec((PROJ_TM, width), tok),
            pl.BlockSpec((PROJ_TM, width), tok),
            pl.BlockSpec((PROJ_TM, width), lambda i, j: (i, jnp.maximum(j - 3, 0))),
            pl.BlockSpec((PROJ_TM, IDX_DIM), tok),
        ],
        out_shape=[
            jax.ShapeDtypeStruct((n, width), BF16),
            jax.ShapeDtypeStruct((n, width), BF16),
            jax.ShapeDtypeStruct((n, width), BF16),
            jax.ShapeDtypeStruct((n, ngate), BF16),
            jax.ShapeDtypeStruct((n, IDX_DIM), BF16),
        ],
        scratch_shapes=[pltpu.VMEM((PROJ_TM, width), F32)],
        compiler_params=pltpu.CompilerParams(
            dimension_semantics=("parallel", "arbitrary"), vmem_limit_bytes=VMEM_LIMIT),
        name="proj_nat",
    )(h2, w_in, w_kidx, w_gate, ln_g.reshape(1, width), ln_b.reshape(1, width))


def _proj_t_body(h_ref, wt_ref, wwt_ref, qt_ref, vt_ref, qit_ref, wit_ref):
    j = pl.program_id(2)
    ch = PROJ_CH
    nc = PROJ_BN // ch
    rows = lambda c: slice(c * ch, (c + 1) * ch)
    matmul = lambda c: _dot_nt(wt_ref[rows(c), :], h_ref[0])

    @pl.when(j == 0)
    def _():
        wit_ref[0] = _dot_nt(wwt_ref[...], h_ref[0])

        def store_q(c, zt):
            qt_ref[0, rows(c), :] = (zt * (ATT_HEAD_DIM ** -0.5 * LOG2E)).astype(BF16)
        _chunked(nc, matmul, store_q)

    @pl.when(j == 1)
    def _():
        def store_v(c, zt):
            for cc in range(PROJ_TM // ATT_KU):
                vt_ref[0, cc, rows(c), :] = zt[:, cc * ATT_KU:(cc + 1) * ATT_KU].astype(BF16)
        _chunked(nc, matmul, store_v)

    @pl.when(j == 2)
    def _():
        def store_qi(c, zt):
            qit_ref[0, rows(c), :] = zt.astype(BF16)
        _chunked(nc, matmul, store_qi)


def _proj_t(h3, wt_cat, wwt, layer):
    b, s, d = h3.shape
    width = PROJ_BN
    nch = PROJ_TM // ATT_KU
    feat = lambda bi, si, j: (bi, 0, si)
    return pl.pallas_call(
        _proj_t_body,
        grid=(b, s // PROJ_TM, 3),
        in_specs=[
            pl.BlockSpec((1, PROJ_TM, d), lambda bi, si, j: (bi, si, 0)),
            pl.BlockSpec((None, width, d), lambda bi, si, j: (layer, j, 0)),
            pl.BlockSpec((None, IDX_HEADS, d), lambda bi, si, j: (layer, 0, 0)),
        ],
        out_specs=[
            pl.BlockSpec((1, width, PROJ_TM), feat),
            pl.BlockSpec((1, nch, width, ATT_KU), lambda bi, si, j: (bi, si, 0, 0)),
            pl.BlockSpec((1, width, PROJ_TM), feat),
            pl.BlockSpec((1, IDX_HEADS, PROJ_TM), feat),
        ],
        out_shape=[
            jax.ShapeDtypeStruct((b, width, s), BF16),
            jax.ShapeDtypeStruct((b, s // ATT_KU, width, ATT_KU), BF16),
            jax.ShapeDtypeStruct((b, width, s), BF16),
            jax.ShapeDtypeStruct((b, IDX_HEADS, s), F32),
        ],
        compiler_params=pltpu.CompilerParams(
            dimension_semantics=("parallel", "parallel", "arbitrary"), vmem_limit_bytes=VMEM_LIMIT),
        name="proj_t",
    )(h3, wt_cat, wwt)


N_NEAR_UNITS = 2


def _bias_body(rb_ref, o_ref):
    h = pl.program_id(0)
    r = lax.broadcasted_iota(I32, (ATT_KU, ATT_TQ), 0)
    c = lax.broadcasted_iota(I32, (ATT_KU, ATT_TQ), 1)
    max_exact = NUM_BUCKETS // 2
    far = rb_ref[NUM_BUCKETS - 1, h]
    for t in range(N_NEAR_UNITS):
        dist = jnp.maximum(t * ATT_KU + c - r, 0)
        nf = jnp.maximum(dist, 1).astype(F32)
        large = max_exact + (jnp.log(nf / max_exact) / math.log(MAX_DISTANCE / max_exact)
                             * (NUM_BUCKETS - max_exact)).astype(I32)
        large = jnp.minimum(large, NUM_BUCKETS - 1)
        bucket = jnp.where(dist < max_exact, dist, large)
        val = jnp.zeros((ATT_KU, ATT_TQ), F32)
        for bkt in range(NUM_BUCKETS):
            val = jnp.where(bucket == bkt, rb_ref[bkt, h], val)
        o_ref[0, t] = (val - far) * LOG2E


def _bias_tiles(rel_bias):
    assert N_NEAR_UNITS * ATT_KU - (ATT_KU - 1) >= MAX_DISTANCE
    return pl.pallas_call(
        _bias_body,
        grid=(ATT_HEADS,),
        in_specs=[pl.BlockSpec(memory_space=pltpu.SMEM)],
        out_specs=pl.BlockSpec((1, N_NEAR_UNITS, ATT_KU, ATT_TQ), lambda h: (h, 0, 0, 0)),
        out_shape=jax.ShapeDtypeStruct((ATT_HEADS, N_NEAR_UNITS, ATT_KU, ATT_TQ), F32),
        name="bias_tiles",
    )(rel_bias)


def _tree_sum(xs):
    while len(xs) > 1:
        xs = [a + b for a, b in zip(xs[::2], xs[1::2])] + ([xs[-1]] if len(xs) % 2 else [])
    return xs[0]


def _sublane_allmax(x):
    for shift in (4, 2, 1):
        x = jnp.maximum(x, pltpu.roll(x, shift, axis=0))
    return x


def _attn_body(kidx_ref, qit_ref, wit_ref, k_ref, vt_ref, qt_ref, bias_ref, o_ref,
               key_scr, hi_scr, lo_scr, m_scr, acc_scr, *, top_k):
    i = pl.program_id(1)
    ku, tq, sl, pk = ATT_KU, ATT_TQ, SUBLANES, BF16_TILE_ROWS
    n_units = i + 1
    q0 = i * tq

    w_all = wit_ref[0] * (IDX_HEADS ** -0.5 * IDX_DIM ** -0.5)
    row = lax.broadcasted_iota(I32, (ku, tq), 0)
    col = lax.broadcasted_iota(I32, (ku, tq), 1)

    def score_unit(u, carry):
        r0 = pl.multiple_of(u * ku, ku)
        kch = kidx_ref[0, pl.ds(r0, ku), :]
        acc = jnp.zeros((ku, tq), F32)
        for h in range(IDX_HEADS):
            d = _dot(kch, qit_ref[0, h * IDX_DIM:(h + 1) * IDX_DIM, :])
            acc = acc + jnp.maximum(d, 0.0) * w_all[h:h + 1, :]
        bits = lax.bitcast_convert_type(acc, I32)
        key = jnp.where(bits < 0, bits ^ jnp.int32(0x7FFFFFFF), bits)
        key = jnp.where(r0 + row <= q0 + col, key, INT_MIN)
        key_scr[pl.ds(r0, ku), :] = key
        hi_scr[pl.ds(r0, ku), :] = lax.shift_right_arithmetic(key, 16).astype(I16)
        lo_scr[pl.ds(r0, ku), :] = ((key & 0xFFFF) - I16_BIAS).astype(I16)
        return carry

    lax.fori_loop(0, n_units, score_unit, 0)

    one_b, zero_b = jnp.ones((), BF16), jnp.zeros((), BF16)

    def count_ge(ref, cand):
        cand16 = jnp.broadcast_to(cand, (pk, tq)).astype(I16)

        def body(u, cnt):
            r0 = pl.multiple_of(u * ku, ku)
            blk = ref[pl.ds(r0, ku), :].reshape(ku // pk, pk, tq)
            hit = jnp.where(blk >= cand16[None], one_b, zero_b)
            return cnt + _tree_sum([hit[g] for g in range(ku // pk)]).astype(F32)

        cnt = lax.fori_loop(0, n_units, body, jnp.zeros((pk, tq), F32))
        return jnp.sum(cnt, axis=0, keepdims=True)

    def kth_largest(ref, kvec):
        zero = jnp.zeros((1, tq), I32)
        prefix = jnp.where(count_ge(ref, zero) >= kvec, zero, I16_MIN)

        def descend(t, prefix):
            cand = prefix | jnp.left_shift(jnp.int32(1), 14 - t)
            return jnp.where(count_ge(ref, cand) >= kvec, cand, prefix)

        return lax.fori_loop(0, 15, descend, prefix)

    def count_gt(ref, v):
        return jnp.where(v == I16_MAX, 0.0, count_ge(ref, jnp.minimum(v + 1, I16_MAX)))

    k_f = jnp.full((1, tq), top_k, F32)
    thr_hi = kth_largest(hi_scr, k_f)
    k_low = k_f - count_gt(hi_scr, thr_hi)
    thr_hi16 = jnp.broadcast_to(thr_hi, (pk, tq)).astype(I16)

    def keep_low_of_winners(u, carry):
        r0 = pl.multiple_of(u * ku, ku)
        hi = hi_scr[pl.ds(r0, ku), :].reshape(ku // pk, pk, tq)
        lo = lo_scr[pl.ds(r0, ku), :].reshape(ku // pk, pk, tq)
        lo_scr[pl.ds(r0, ku), :] = jnp.where(hi == thr_hi16[None], lo, jnp.int16(I16_MIN)).reshape(ku, tq)
        return carry

    lax.fori_loop(0, n_units, keep_low_of_winners, 0)
    thr_lo = kth_largest(lo_scr, k_low)
    thr_raw = thr_hi * (1 << 16) + (thr_lo + I16_BIAS)
    thr = jnp.maximum(thr_raw, INT_MIN + 1)

    n_greater = count_gt(lo_scr, thr_lo)
    n_tied_ok = k_low - n_greater
    n_tied = count_ge(lo_scr, thr_lo) - n_greater
    has_excess = jnp.logical_and(n_tied > n_tied_ok, thr_raw > INT_MIN)

    @pl.when(jnp.max(jnp.where(has_excess, 1.0, 0.0)) > 0.0)
    def _():
        earlier = (lax.broadcasted_iota(I32, (ku, ku), 0) > lax.broadcasted_iota(I32, (ku, ku), 1))
        earlier = jnp.where(earlier, 1.0, 0.0).astype(BF16)

        def demote(u, seen):
            r0 = pl.multiple_of(u * ku, ku)
            key = key_scr[pl.ds(r0, ku), :]
            tied = key == thr
            rank = _dot(earlier, jnp.where(tied, 1.0, 0.0).astype(BF16)) + seen
            key_scr[pl.ds(r0, ku), :] = jnp.where(jnp.logical_and(tied, rank >= n_tied_ok), INT_MIN, key)
            return seen + jnp.sum(jnp.where(tied, 1.0, 0.0), axis=0, keepdims=True)

        lax.fori_loop(0, n_units, demote, jnp.zeros((1, tq), F32))

    m_scr[...] = jnp.full(m_scr.shape, MASKED_LOGIT, F32)
    acc_scr[...] = jnp.zeros(acc_scr.shape, F32)
    ones_rows = jnp.ones((ACC_ROWS - ATT_HEAD_DIM, ku), BF16)

    def unit(u, near_tile):
        r0 = pl.multiple_of(u * ku, ku)
        keep = key_scr[pl.ds(r0, ku), :].reshape(ku // sl, sl, tq) >= thr[None]
        mask_add = jnp.where(keep, 0.0, MASKED_LOGIT)

        def qk(h):
            hs = slice(h * ATT_HEAD_DIM, (h + 1) * ATT_HEAD_DIM)
            s = _dot(k_ref[0, pl.ds(r0, ku), hs], qt_ref[0, hs, :])
            if near_tile is not None:
                s = s + bias_ref[h, near_tile]
            return s

        pending = [qk(h) for h in range(min(QK_AHEAD, ATT_HEADS))]
        for h in range(ATT_HEADS):
            hs = slice(h * ATT_HEAD_DIM, (h + 1) * ATT_HEAD_DIM)
            s = pending.pop(0)
            if h + QK_AHEAD < ATT_HEADS:
                pending.append(qk(h + QK_AHEAD))
            s = s.reshape(ku // sl, sl, tq) + mask_add
            m_old = m_scr[h]
            m_new = jnp.maximum(m_old, _sublane_allmax(jnp.max(s, axis=0)))
            alpha = jnp.exp2(m_old - m_new)
            p = jnp.exp2(s - m_new[None])
            m_scr[h] = m_new
            v_ext = jnp.concatenate([vt_ref[0, u, hs, :], ones_rows], axis=0)
            pv = _dot(v_ext, p.reshape(ku, tq).astype(BF16))
            acc = acc_scr[h].reshape(ACC_ROWS // sl, sl, tq) * alpha[None]
            acc_scr[h] = pv + acc.reshape(ACC_ROWS, tq)

    lax.fori_loop(0, n_units - N_NEAR_UNITS, lambda u, c: (unit(u, None), c)[1], 0)
    for t in range(N_NEAR_UNITS - 1, -1, -1):
        lax.fori_loop(0, jnp.minimum(n_units - t, 1), lambda _, c, t=t: (unit(i - t, t), c)[1], 0)

    for h in range(ATT_HEADS):
        hs = slice(h * ATT_HEAD_DIM, (h + 1) * ATT_HEAD_DIM)
        l = acc_scr[h, ATT_HEAD_DIM:ATT_HEAD_DIM + 1, :]
        o_ref[0, :, hs] = (acc_scr[h, :ATT_HEAD_DIM, :] / l).T.astype(BF16)


def _attention(kidx, qit, wit, k, vt, qt, bias, top_k):
    b, s, width = k.shape
    one = pl.Buffered(1)
    return pl.pallas_call(
        functools.partial(_attn_body, top_k=top_k),
        grid=(b, s // ATT_TQ),
        in_specs=[
            pl.BlockSpec((1, s, IDX_DIM), lambda bi, i: (bi, 0, 0), pipeline_mode=one),
            pl.BlockSpec((1, IDX_HEADS * IDX_DIM, ATT_TQ), lambda bi, i: (bi, 0, i)),
            pl.BlockSpec((1, IDX_HEADS, ATT_TQ), lambda bi, i: (bi, 0, i)),
            pl.BlockSpec((1, s, width), lambda bi, i: (bi, 0, 0), pipeline_mode=one),
            pl.BlockSpec((1, s // ATT_KU, width, ATT_KU), lambda bi, i: (bi, 0, 0, 0), pipeline_mode=one),
            pl.BlockSpec((1, width, ATT_TQ), lambda bi, i: (bi, 0, i)),
            pl.BlockSpec(bias.shape, lambda bi, i: (0, 0, 0, 0), pipeline_mode=one),
        ],
        out_specs=pl.BlockSpec((1, ATT_TQ, width), lambda bi, i: (bi, i, 0)),
        out_shape=jax.ShapeDtypeStruct((b, s, width), BF16),
        scratch_shapes=[
            pltpu.VMEM((s, ATT_TQ), I32),
            pltpu.VMEM((s, ATT_TQ), I16),
            pltpu.VMEM((s, ATT_TQ), I16),
            pltpu.VMEM((ATT_HEADS, SUBLANES, ATT_TQ), F32),
            pltpu.VMEM((ATT_HEADS, ACC_ROWS, ATT_TQ), F32),
        ],
        compiler_params=pltpu.CompilerParams(
            dimension_semantics=("parallel", "arbitrary"), vmem_limit_bytes=VMEM_LIMIT),
        name="dsa_attention",
    )(kidx, qit, wit, k, vt, qt, bias)


def _merge_body(x_ref, u_ref, vln_ref, yb_ref, gate_ref, ws_ref, bs_ref, wa_ref, wb_ref, wo_ref,
                gpost_ref, o_ref, ya_scr):
    ch = SGU_CHUNK
    d = x_ref.shape[1]
    tril = (lax.broadcasted_iota(I32, (ch, ch), 0) >= lax.broadcasted_iota(I32, (ch, ch), 1))
    for g in range(SGU_GROUPS):
        gs = slice(g * ch, (g + 1) * ch)
        wsg = jnp.where(tril, ws_ref[g], 0.0).astype(BF16)
        for c in range(MERGE_TM // ch):
            cs = slice(c * ch, (c + 1) * ch)
            mixed = _dot(wsg, vln_ref[cs, gs]) + bs_ref[g]
            ya_scr[cs, gs] = (u_ref[cs, gs].astype(F32) * mixed).astype(BF16)
    ma = _dot(ya_scr[...], wa_ref[...])
    mb = _dot(yb_ref[...], wb_ref[...])
    merged = gate_ref[:, :d].astype(F32) * ma + gate_ref[:, d:].astype(F32) * mb
    o = _dot(merged.astype(BF16), wo_ref[...])
    o_ref[...] = x_ref[...] + _rms(o, gpost_ref[...])


def _merge(x2, u, vln, yb, gate, w_s, b_s, w_a, w_b, w_o, g_post, layer):
    n, d = x2.shape
    width = u.shape[1]
    one = pl.Buffered(1)
    tok = lambda i: (i, 0)
    const2 = lambda i: (0, 0)
    whole = lambda a: pl.BlockSpec((None,) + a.shape[1:], lambda i: (layer,) + (0,) * (a.ndim - 1),
                                   pipeline_mode=one)
    return pl.pallas_call(
        _merge_body,
        grid=(n // MERGE_TM,),
        in_specs=[
            pl.BlockSpec((MERGE_TM, d), tok),
            pl.BlockSpec((MERGE_TM, width), tok),
            pl.BlockSpec((MERGE_TM, width), tok),
            pl.BlockSpec((MERGE_TM, width), tok),
            pl.BlockSpec((MERGE_TM, 2 * d), tok),
            whole(w_s), whole(b_s), whole(w_a), whole(w_b), whole(w_o),
            pl.BlockSpec((1, d), const2),
        ],
        out_specs=pl.BlockSpec((MERGE_TM, d), tok),
        out_shape=jax.ShapeDtypeStruct((n, d), F32),
        scratch_shapes=[pltpu.VMEM((MERGE_TM, width), BF16)],
        compiler_params=pltpu.CompilerParams(
            dimension_semantics=("parallel",), vmem_limit_bytes=VMEM_LIMIT),
        name="merge",
    )(x2, u, vln, yb, gate, w_s, b_s, w_a, w_b, w_o, g_post.reshape(1, d))


def kernel(x, ffn1_norm_pre, ffn1_norm_post, ffn1_w_in, ffn1_w_out, mix_norm_pre, mix_norm_post, w_in,
           sgu_ln_g, sgu_ln_b, sgu_w_s, sgu_b, rel_bias, w_branch_a, w_branch_b, w_gate, w_out,
           ffn2_norm_pre, ffn2_norm_post, ffn2_w_in, ffn2_w_out):
    b, s, d = x.shape
    depth = w_in.shape[0]
    sgu_w = sgu_ln_g.shape[1]
    att_w = ATT_HEADS * ATT_HEAD_DIM
    idx_w = IDX_HEADS * IDX_DIM
    top_k = min(TOPK_MAX, s // 4)
    assert s % PROJ_TM == 0 and s % ATT_TQ == 0 and (b * s) % FFN_TM == 0
    assert sgu_w == PROJ_BN and att_w == PROJ_BN and idx_w == PROJ_BN and ATT_KU == ATT_TQ

    sizes = (sgu_w, sgu_w, att_w, att_w, att_w, idx_w, IDX_DIM, IDX_HEADS)
    offs = [0]
    for sz in sizes:
        offs.append(offs[-1] + sz)
    col = lambda w, idx: w[:, :, offs[idx]:offs[idx + 1]]
    bf = lambda w: w.astype(BF16)
    w_gate_b = bf(w_gate)
    w_nat = jnp.concatenate([bf(col(w_in, 0)), bf(col(w_in, 1)), bf(col(w_in, 3))], axis=2)
    nat_blocks = (0, 1, 2)
    wt_cat = jnp.swapaxes(jnp.concatenate([bf(col(w_in, 2)), bf(col(w_in, 4)), bf(col(w_in, 5))], axis=2), 1, 2)
    wwt = jnp.swapaxes(bf(col(w_in, 7)), 1, 2)
    w_kidx = bf(col(w_in, 6))
    ffn1_in_b, ffn1_out_b, ffn2_in_b, ffn2_out_b = bf(ffn1_w_in), bf(ffn1_w_out), bf(ffn2_w_in), bf(ffn2_w_out)
    w_a_b, w_b_b, w_o_b = bf(w_branch_a), bf(w_branch_b), bf(w_out)
    b_s = sgu_b.reshape(depth, SGU_GROUPS, SGU_CHUNK, 1)

    bias = _bias_tiles(rel_bias)
    x2 = x.reshape(b * s, d)
    for l in range(depth):
        x2, h2 = _ffn(x2, ffn1_norm_pre[l], ffn1_norm_post[l], ffn1_in_b, ffn1_out_b, l,
                      g_next=mix_norm_pre[l])
        u, vln, k, gate, kidx = _proj_nat(h2, w_nat, w_kidx, w_gate_b, sgu_ln_g[l], sgu_ln_b[l], l,
                                          nat_blocks)
        qt, vt, qit, wit = _proj_t(h2.reshape(b, s, d), wt_cat, wwt, l)
        yb = _attention(kidx.reshape(b, s, IDX_DIM), qit, wit, k.reshape(b, s, att_w), vt, qt, bias, top_k)
        x2 = _merge(x2, u, vln, yb.reshape(b * s, att_w), gate,
                    sgu_w_s, b_s, w_a_b, w_b_b, w_o_b, mix_norm_post[l], l)
        (x2,) = _ffn(x2, ffn2_norm_pre[l], ffn2_norm_post[l], ffn2_in_b, ffn2_out_b, l)
    return x2.reshape(b, s, d)
```

```python
import functools
import math

import jax
import jax.numpy as jnp
from jax import lax
from jax.experimental import pallas as pl
from jax.experimental.pallas import tpu as pltpu

F32 = jnp.float32
BF16 = jnp.bfloat16
I32 = jnp.int32
I16 = jnp.int16

SGU_GROUPS = 8
SGU_CHUNK = 128
ATT_HEADS = 8
ATT_HEAD_DIM = 128
IDX_HEADS = 16
IDX_DIM = 64
TOPK_MAX = 256
NUM_BUCKETS = 32
MAX_DISTANCE = 128
NORM_EPS = 1e-6
LN_EPS = 1e-5

V7X_VMEM_BYTES = 64 * 1024 * 1024
VMEM_LIMIT = V7X_VMEM_BYTES - 8 * 1024 * 1024
SUBLANES = 8
BF16_TILE_ROWS = 16

FFN_TM = 512
FFN_TF = 512
PROJ_TM = 1024
PROJ_BN = 1024
PROJ_CH = 256
MM_AHEAD = 4
MERGE_TM = 256
ATT_TQ = 256
ATT_KU = 256
ACC_ROWS = ATT_HEAD_DIM + BF16_TILE_ROWS
QK_AHEAD = ATT_HEADS
FIXED_MAX_DENOM_LIMIT = 2.0 ** 20

INT_MIN = -(2 ** 31)
I16_MIN, I16_MAX, I16_BIAS = -(2 ** 15), 2 ** 15 - 1, 2 ** 15
MASKED_LOGIT = -1e30
LOG2E = math.log2(math.e)


def _rms(xf, g):
    ms = jnp.mean(xf * xf, axis=-1, keepdims=True)
    return xf * lax.rsqrt(ms + NORM_EPS) * g


def _gelu_tanh(x):
    c = math.sqrt(2.0 / math.pi)
    return x * (0.5 * (1.0 + jnp.tanh(c * (x + 0.044715 * (x * x * x)))))


def _dot(a, b):
    return jnp.dot(a, b, preferred_element_type=F32)


def _dot_nt(a, b):
    return lax.dot_general(a, b, (((1,), (1,)), ((), ())), preferred_element_type=F32)


def _ffn_body(*refs, emit_next):
    if emit_next:
        x_ref, gpre_ref, gpost_ref, gnext_ref, wa_ref, wb_ref, wo_ref, o_ref, hn_ref, h_scr, acc_scr = refs
    else:
        x_ref, gpre_ref, gpost_ref, wa_ref, wb_ref, wo_ref, o_ref, h_scr, acc_scr = refs
    j = pl.program_id(1)

    @pl.when(j == 0)
    def _():
        h_scr[...] = _rms(x_ref[...], gpre_ref[...]).astype(BF16)
        acc_scr[...] = jnp.zeros_like(acc_scr)

    h = h_scr[...]
    a = _dot(h, wa_ref[...])
    b = _dot(h, wb_ref[...])
    g = (a * jax.nn.sigmoid(a) * b).astype(BF16)
    acc_scr[...] += _dot(g, wo_ref[...])

    @pl.when(j == pl.num_programs(1) - 1)
    def _():
        y = x_ref[...] + 0.5 * _rms(acc_scr[...], gpost_ref[...])
        o_ref[...] = y
        if emit_next:
            hn_ref[...] = _rms(y, gnext_ref[...]).astype(BF16)


def _ffn(x2, g_pre, g_post, w_in, w_out, layer, g_next=None):
    n, d = x2.shape
    d_ff = w_out.shape[1]
    nf = d_ff // FFN_TF
    emit_next = g_next is not None
    vec = pl.BlockSpec((1, d), lambda i, j: (0, 0))
    tok = pl.BlockSpec((FFN_TM, d), lambda i, j: (i, 0))
    gains = [g_pre, g_post] + ([g_next] if emit_next else [])
    out_shape = [jax.ShapeDtypeStruct((n, d), F32)] + ([jax.ShapeDtypeStruct((n, d), BF16)] if emit_next else [])
    return pl.pallas_call(
        functools.partial(_ffn_body, emit_next=emit_next),
        grid=(n // FFN_TM, nf),
        in_specs=[tok] + [vec] * len(gains) + [
            pl.BlockSpec((None, d, FFN_TF), lambda i, j: (layer, 0, j)),
            pl.BlockSpec((None, d, FFN_TF), lambda i, j: (layer, 0, j + nf)),
            pl.BlockSpec((None, FFN_TF, d), lambda i, j: (layer, j, 0)),
        ],
        out_specs=[tok] * len(out_shape),
        out_shape=out_shape,
        scratch_shapes=[pltpu.VMEM((FFN_TM, d), BF16), pltpu.VMEM((FFN_TM, d), F32)],
        compiler_params=pltpu.CompilerParams(
            dimension_semantics=("parallel", "arbitrary"), vmem_limit_bytes=VMEM_LIMIT),
        name="ffn",
    )(x2, *[g.reshape(1, d) for g in gains], w_in, w_in, w_out)


def _chunked(n_chunks, matmul, epilogue):
    z = [matmul(c) for c in range(min(MM_AHEAD, n_chunks))]
    for c in range(n_chunks):
        if c + MM_AHEAD < n_chunks:
            z.append(matmul(c + MM_AHEAD))
        epilogue(c, z[c])
        z[c] = None


def _proj_nat_body(h_ref, w_ref, wki_ref, wg_ref, lng_ref, lnb_ref,
                   u_ref, vln_ref, k_ref, gate_ref, kidx_ref, v_scr):
    j = pl.program_id(1)
    ch = PROJ_CH
    nc = PROJ_BN // ch
    cols = lambda c: slice(c * ch, (c + 1) * ch)
    matmul = lambda c: _dot(h_ref[...], w_ref[:, cols(c)])
    matmul_gate = lambda c: _dot(h_ref[...], wg_ref[:, cols(c)])

    @pl.when(j == 0)
    def _():
        kidx_ref[...] = _dot(h_ref[...], wki_ref[...]).astype(BF16)

        def store_u(c, z):
            u_ref[:, cols(c)] = _gelu_tanh(z).astype(BF16)
        _chunked(nc, matmul, store_u)

    @pl.when(j == 1)
    def _():
        def store_v(c, z):
            v_scr[:, cols(c)] = _gelu_tanh(z)
        _chunked(nc, matmul, store_v)
        v = v_scr[...]
        mu = jnp.mean(v, axis=-1, keepdims=True)
        vc = v - mu
        var = jnp.mean(vc * vc, axis=-1, keepdims=True)
        vln_ref[...] = (vc * lax.rsqrt(var + LN_EPS) * lng_ref[...] + lnb_ref[...]).astype(BF16)

    @pl.when(j == 2)
    def _():
        def store_k(c, z):
            k_ref[:, cols(c)] = z.astype(BF16)
        _chunked(nc, matmul, store_k)

    @pl.when(j >= 3)
    def _():
        def store_gate(c, z):
            gate_ref[:, cols(c)] = jax.nn.sigmoid(z).astype(BF16)
        _chunked(nc, matmul_gate, store_gate)


def _proj_nat(h2, w_in, w_kidx, w_gate, ln_g, ln_b, layer, blocks):
    n, d = h2.shape
    width = PROJ_BN
    ngate = w_gate.shape[2]
    n_in = len(blocks)
    assert n_in == 3
    nj = n_in + ngate // width
    tok = lambda i, j: (i, 0)

    def in_block(i, j):
        blk = blocks[n_in - 1]
        for t in range(n_in - 2, -1, -1):
            blk = jnp.where(j == t, blocks[t], blk)
        return (layer, 0, blk)

    return pl.pallas_call(
        _proj_nat_body,
        grid=(n // PROJ_TM, nj),
        in_specs=[
            pl.BlockSpec((PROJ_TM, d), tok),
            pl.BlockSpec((None, d, width), in_block),
            pl.BlockSpec((None, d, IDX_DIM), lambda i, j: (layer, 0, 0)),
            pl.BlockSpec((None, d, width), lambda i, j: (layer, 0, jnp.maximum(j - n_in, 0))),
            pl.BlockSpec((1, width), lambda i, j: (0, 0)),
            pl.BlockSpec((1, width), lambda i, j: (0, 0)),
        ],
        out_specs=[
            pl.BlockSpec((PROJ_TM, width), tok),
            pl.BlockSpec((PROJ_TM, width), tok),
            pl.BlockSpec((PROJ_TM, width), tok),
            pl.BlockSpec((PROJ_TM, width), lambda i, j: (i, jnp.maximum(j - 3, 0))),
            pl.BlockSpec((PROJ_TM, IDX_DIM), tok),
        ],
        out_shape=[
            jax.ShapeDtypeStruct((n, width), BF16),
            jax.ShapeDtypeStruct((n, width), BF16),
            jax.ShapeDtypeStruct((n, width), BF16),
            jax.ShapeDtypeStruct((n, ngate), BF16),
            jax.ShapeDtypeStruct((n, IDX_DIM), BF16),
        ],
        scratch_shapes=[pltpu.VMEM((PROJ_TM, width), F32)],
        compiler_params=pltpu.CompilerParams(
            dimension_semantics=("parallel", "arbitrary"), vmem_limit_bytes=VMEM_LIMIT),
        name="proj_nat",
    )(h2, w_in, w_kidx, w_gate, ln_g.reshape(1, width), ln_b.reshape(1, width))


def _proj_t_body(h_ref, wt_ref, wwt_ref, qt_ref, vt_ref, qit_ref, wit_ref):
    j = pl.program_id(2)
    ch = PROJ_CH
    nc = PROJ_BN // ch
    rows = lambda c: slice(c * ch, (c + 1) * ch)
    matmul = lambda c: _dot_nt(wt_ref[rows(c), :], h_ref[0])

    @pl.when(j == 0)
    def _():
        wit_ref[0] = _dot_nt(wwt_ref[...], h_ref[0])

        def store_q(c, zt):
            qt_ref[0, rows(c), :] = (zt * (ATT_HEAD_DIM ** -0.5 * LOG2E)).astype(BF16)
        _chunked(nc, matmul, store_q)

    @pl.when(j == 1)
    def _():
        def store_v(c, zt):
            for cc in range(PROJ_TM // ATT_KU):
                vt_ref[0, cc, rows(c), :] = zt[:, cc * ATT_KU:(cc + 1) * ATT_KU].astype(BF16)
        _chunked(nc, matmul, store_v)

    @pl.when(j == 2)
    def _():
        def store_qi(c, zt):
            qit_ref[0, rows(c), :] = zt.astype(BF16)
        _chunked(nc, matmul, store_qi)


def _proj_t(h3, wt_cat, wwt, layer):
    b, s, d = h3.shape
    width = PROJ_BN
    nch = PROJ_TM // ATT_KU
    feat = lambda bi, si, j: (bi, 0, si)
    return pl.pallas_call(
        _proj_t_body,
        grid=(b, s // PROJ_TM, 3),
        in_specs=[
            pl.BlockSpec((1, PROJ_TM, d), lambda bi, si, j: (bi, si, 0)),
            pl.BlockSpec((None, width, d), lambda bi, si, j: (layer, j, 0)),
            pl.BlockSpec((None, IDX_HEADS, d), lambda bi, si, j: (layer, 0, 0)),
        ],
        out_specs=[
            pl.BlockSpec((1, width, PROJ_TM), feat),
            pl.BlockSpec((1, nch, width, ATT_KU), lambda bi, si, j: (bi, si, 0, 0)),
            pl.BlockSpec((1, width, PROJ_TM), feat),
            pl.BlockSpec((1, IDX_HEADS, PROJ_TM), feat),
        ],
        out_shape=[
            jax.ShapeDtypeStruct((b, width, s), BF16),
            jax.ShapeDtypeStruct((b, s // ATT_KU, width, ATT_KU), BF16),
            jax.ShapeDtypeStruct((b, width, s), BF16),
            jax.ShapeDtypeStruct((b, IDX_HEADS, s), F32),
        ],
        compiler_params=pltpu.CompilerParams(
            dimension_semantics=("parallel", "parallel", "arbitrary"), vmem_limit_bytes=VMEM_LIMIT),
        name="proj_t",
    )(h3, wt_cat, wwt)


N_NEAR_UNITS = 2


def _bias_body(rb_ref, o_ref):
    h = pl.program_id(0)
    r = lax.broadcasted_iota(I32, (ATT_KU, ATT_TQ), 0)
    c = lax.broadcasted_iota(I32, (ATT_KU, ATT_TQ), 1)
    max_exact = NUM_BUCKETS // 2
    far = rb_ref[NUM_BUCKETS - 1, h]
    for t in range(N_NEAR_UNITS):
        dist = jnp.maximum(t * ATT_KU + c - r, 0)
        nf = jnp.maximum(dist, 1).astype(F32)
        large = max_exact + (jnp.log(nf / max_exact) / math.log(MAX_DISTANCE / max_exact)
                             * (NUM_BUCKETS - max_exact)).astype(I32)
        large = jnp.minimum(large, NUM_BUCKETS - 1)
        bucket = jnp.where(dist < max_exact, dist, large)
        val = jnp.zeros((ATT_KU, ATT_TQ), F32)
        for bkt in range(NUM_BUCKETS):
            val = jnp.where(bucket == bkt, rb_ref[bkt, h], val)
        o_ref[0, t] = (val - far) * LOG2E


def _bias_tiles(rel_bias):
    assert N_NEAR_UNITS * ATT_KU - (ATT_KU - 1) >= MAX_DISTANCE
    return pl.pallas_call(
        _bias_body,
        grid=(ATT_HEADS,),
        in_specs=[pl.BlockSpec(memory_space=pltpu.SMEM)],
        out_specs=pl.BlockSpec((1, N_NEAR_UNITS, ATT_KU, ATT_TQ), lambda h: (h, 0, 0, 0)),
        out_shape=jax.ShapeDtypeStruct((ATT_HEADS, N_NEAR_UNITS, ATT_KU, ATT_TQ), F32),
        name="bias_tiles",
    )(rel_bias)


def _tree_sum(xs):
    while len(xs) > 1:
        xs = [a + b for a, b in zip(xs[::2], xs[1::2])] + ([xs[-1]] if len(xs) % 2 else [])
    return xs[0]


def _sublane_allmax(x):
    for shift in (4, 2, 1):
        x = jnp.maximum(x, pltpu.roll(x, shift, axis=0))
    return x


def _attn_body(kidx_ref, qit_ref, wit_ref, k_ref, vt_ref, qt_ref, bias_ref, o_ref,
               key_scr, hi_scr, lo_scr, m_scr, acc_scr, save_scr, *, top_k):
    i = pl.program_id(1)
    ku, tq, sl, pk = ATT_KU, ATT_TQ, SUBLANES, BF16_TILE_ROWS
    n_units = i + 1
    q0 = i * tq

    w_all = wit_ref[0] * (IDX_HEADS ** -0.5 * IDX_DIM ** -0.5)
    row = lax.broadcasted_iota(I32, (ku, tq), 0)
    col = lax.broadcasted_iota(I32, (ku, tq), 1)

    def score_unit(u, carry):
        r0 = pl.multiple_of(u * ku, ku)
        kch = kidx_ref[0, pl.ds(r0, ku), :]
        acc = jnp.zeros((ku, tq), F32)
        for h in range(IDX_HEADS):
            d = _dot(kch, qit_ref[0, h * IDX_DIM:(h + 1) * IDX_DIM, :])
            acc = acc + jnp.maximum(d, 0.0) * w_all[h:h + 1, :]
        bits = lax.bitcast_convert_type(acc, I32)
        key = jnp.where(bits < 0, bits ^ jnp.int32(0x7FFFFFFF), bits)
        key = jnp.where(r0 + row <= q0 + col, key, INT_MIN)
        key_scr[pl.ds(r0, ku), :] = key
        hi_scr[pl.ds(r0, ku), :] = lax.shift_right_arithmetic(key, 16).astype(I16)
        lo_scr[pl.ds(r0, ku), :] = ((key & 0xFFFF) - I16_BIAS).astype(I16)
        return carry

    lax.fori_loop(0, n_units, score_unit, 0)

    one_b, zero_b = jnp.ones((), BF16), jnp.zeros((), BF16)

    def count_ge(ref, cand):
        cand16 = jnp.broadcast_to(cand, (pk, tq)).astype(I16)

        def body(u, cnt):
            r0 = pl.multiple_of(u * ku, ku)
            blk = ref[pl.ds(r0, ku), :].reshape(ku // pk, pk, tq)
            hit = jnp.where(blk >= cand16[None], one_b, zero_b)
            return cnt + _tree_sum([hit[g] for g in range(ku // pk)]).astype(F32)

        cnt = lax.fori_loop(0, n_units, body, jnp.zeros((pk, tq), F32))
        return jnp.sum(cnt, axis=0, keepdims=True)

    def kth_largest(ref, kvec):
        zero = jnp.zeros((1, tq), I32)
        prefix = jnp.where(count_ge(ref, zero) >= kvec, zero, I16_MIN)

        def descend(t, prefix):
            cand = prefix | jnp.left_shift(jnp.int32(1), 14 - t)
            return jnp.where(count_ge(ref, cand) >= kvec, cand, prefix)

        return lax.fori_loop(0, 15, descend, prefix)

    def count_gt(ref, v):
        return jnp.where(v == I16_MAX, 0.0, count_ge(ref, jnp.minimum(v + 1, I16_MAX)))

    k_f = jnp.full((1, tq), top_k, F32)
    thr_hi = kth_largest(hi_scr, k_f)
    k_low = k_f - count_gt(hi_scr, thr_hi)
    thr_hi16 = jnp.broadcast_to(thr_hi, (pk, tq)).astype(I16)

    def keep_low_of_winners(u, carry):
        r0 = pl.multiple_of(u * ku, ku)
        hi = hi_scr[pl.ds(r0, ku), :].reshape(ku // pk, pk, tq)
        lo = lo_scr[pl.ds(r0, ku), :].reshape(ku // pk, pk, tq)
        lo_scr[pl.ds(r0, ku), :] = jnp.where(hi == thr_hi16[None], lo, jnp.int16(I16_MIN)).reshape(ku, tq)
        return carry

    lax.fori_loop(0, n_units, keep_low_of_winners, 0)
    thr_lo = kth_largest(lo_scr, k_low)
    thr_raw = thr_hi * (1 << 16) + (thr_lo + I16_BIAS)
    thr = jnp.maximum(thr_raw, INT_MIN + 1)

    n_greater = count_gt(lo_scr, thr_lo)
    n_tied_ok = k_low - n_greater
    n_tied = count_ge(lo_scr, thr_lo) - n_greater
    has_excess = jnp.logical_and(n_tied > n_tied_ok, thr_raw > INT_MIN)

    @pl.when(jnp.max(jnp.where(has_excess, 1.0, 0.0)) > 0.0)
    def _():
        earlier = (lax.broadcasted_iota(I32, (ku, ku), 0) > lax.broadcasted_iota(I32, (ku, ku), 1))
        earlier = jnp.where(earlier, 1.0, 0.0).astype(BF16)

        def demote(u, seen):
            r0 = pl.multiple_of(u * ku, ku)
            key = key_scr[pl.ds(r0, ku), :]
            tied = key == thr
            rank = _dot(earlier, jnp.where(tied, 1.0, 0.0).astype(BF16)) + seen
            key_scr[pl.ds(r0, ku), :] = jnp.where(jnp.logical_and(tied, rank >= n_tied_ok), INT_MIN, key)
            return seen + jnp.sum(jnp.where(tied, 1.0, 0.0), axis=0, keepdims=True)

        lax.fori_loop(0, n_units, demote, jnp.zeros((1, tq), F32))

    m_scr[...] = jnp.full(m_scr.shape, MASKED_LOGIT, F32)
    acc_scr[...] = jnp.zeros(acc_scr.shape, F32)
    ones_rows = jnp.ones((ACC_ROWS - ATT_HEAD_DIM, ku), BF16)

    def unit(u, near_tile, rescale):
        r0 = pl.multiple_of(u * ku, ku)
        keep = key_scr[pl.ds(r0, ku), :].reshape(ku // sl, sl, tq) >= thr[None]
        mask_add = jnp.where(keep, 0.0, MASKED_LOGIT)

        def qk(h):
            hs = slice(h * ATT_HEAD_DIM, (h + 1) * ATT_HEAD_DIM)
            s = _dot(k_ref[0, pl.ds(r0, ku), hs], qt_ref[0, hs, :])
            if near_tile is not None:
                s = s + bias_ref[h, near_tile]
            return s

        ahead = QK_AHEAD
        pending = [qk(h) for h in range(min(ahead, ATT_HEADS))]
        for h in range(ATT_HEADS):
            hs = slice(h * ATT_HEAD_DIM, (h + 1) * ATT_HEAD_DIM)
            s = pending.pop(0)
            if h + ahead < ATT_HEADS:
                pending.append(qk(h + ahead))
            s = s.reshape(ku // sl, sl, tq) + mask_add
            m_old = m_scr[h]
            if rescale:
                m_new = jnp.maximum(m_old, _sublane_allmax(jnp.max(s, axis=0)))
                alpha = jnp.exp2(m_old - m_new)
                m_scr[h] = m_new
            else:
                m_new = m_old
            p = jnp.exp2(s - m_new[None])
            v_ext = jnp.concatenate([vt_ref[0, u, hs, :], ones_rows], axis=0)
            pv = _dot(v_ext, p.reshape(ku, tq).astype(BF16))
            if rescale:
                acc = acc_scr[h].reshape(ACC_ROWS // sl, sl, tq) * alpha[None]
                acc_scr[h] = pv + acc.reshape(ACC_ROWS, tq)
            else:
                acc_scr[h] = pv + acc_scr[h]

    def run_units(first_far, near_trips, rescale):
        lax.fori_loop(first_far, n_far, lambda u, c: (unit(u, None, rescale), c)[1], 0)
        for t in range(N_NEAR_UNITS - 1, -1, -1):
            lax.fori_loop(0, jnp.minimum(near_trips, jnp.minimum(n_units - t, 1)),
                          lambda _, c, t=t: (unit(i - t, t, rescale), c)[1], 0)

    n_far = jnp.maximum(n_units - N_NEAR_UNITS, 0)
    lead = jnp.minimum(n_far, 1)
    lax.fori_loop(0, lead, lambda u, c: (unit(u, None, True), c)[1], 0)
    save_scr[...] = acc_scr[...]
    run_units(lead, lead, False)
    outgrown = jnp.zeros((1, tq), F32)
    for h in range(ATT_HEADS):
        l = acc_scr[h, ATT_HEAD_DIM:ATT_HEAD_DIM + 1, :]
        outgrown = jnp.maximum(outgrown, jnp.where(l < FIXED_MAX_DENOM_LIMIT, 0.0, 1.0))

    @pl.when(jnp.logical_or(lead == 0, jnp.max(outgrown) > 0.0))
    def _():
        acc_scr[...] = save_scr[...]
        run_units(lead, 1, True)

    for h in range(ATT_HEADS):
        hs = slice(h * ATT_HEAD_DIM, (h + 1) * ATT_HEAD_DIM)
        l = acc_scr[h, ATT_HEAD_DIM:ATT_HEAD_DIM + 1, :]
        o_ref[0, :, hs] = (acc_scr[h, :ATT_HEAD_DIM, :] / l).T.astype(BF16)


def _attention(kidx, qit, wit, k, vt, qt, bias, top_k):
    b, s, width = k.shape
    one = pl.Buffered(1)
    return pl.pallas_call(
        functools.partial(_attn_body, top_k=top_k),
        grid=(b, s // ATT_TQ),
        in_specs=[
            pl.BlockSpec((1, s, IDX_DIM), lambda bi, i: (bi, 0, 0), pipeline_mode=one),
            pl.BlockSpec((1, IDX_HEADS * IDX_DIM, ATT_TQ), lambda bi, i: (bi, 0, i)),
            pl.BlockSpec((1, IDX_HEADS, ATT_TQ), lambda bi, i: (bi, 0, i)),
            pl.BlockSpec((1, s, width), lambda bi, i: (bi, 0, 0), pipeline_mode=one),
            pl.BlockSpec((1, s // ATT_KU, width, ATT_KU), lambda bi, i: (bi, 0, 0, 0), pipeline_mode=one),
            pl.BlockSpec((1, width, ATT_TQ), lambda bi, i: (bi, 0, i)),
            pl.BlockSpec(bias.shape, lambda bi, i: (0, 0, 0, 0), pipeline_mode=one),
        ],
        out_specs=pl.BlockSpec((1, ATT_TQ, width), lambda bi, i: (bi, i, 0)),
        out_shape=jax.ShapeDtypeStruct((b, s, width), BF16),
        scratch_shapes=[
            pltpu.VMEM((s, ATT_TQ), I32),
            pltpu.VMEM((s, ATT_TQ), I16),
            pltpu.VMEM((s, ATT_TQ), I16),
            pltpu.VMEM((ATT_HEADS, SUBLANES, ATT_TQ), F32),
            pltpu.VMEM((ATT_HEADS, ACC_ROWS, ATT_TQ), F32),
            pltpu.VMEM((ATT_HEADS, ACC_ROWS, ATT_TQ), F32),
        ],
        compiler_params=pltpu.CompilerParams(
            dimension_semantics=("parallel", "arbitrary"), vmem_limit_bytes=VMEM_LIMIT),
        name="dsa_attention",
    )(kidx, qit, wit, k, vt, qt, bias)


def _merge_body(x_ref, u_ref, vln_ref, yb_ref, gate_ref, ws_ref, bs_ref, wa_ref, wb_ref, wo_ref,
                gpost_ref, o_ref, ya_scr):
    ch = SGU_CHUNK
    d = x_ref.shape[1]
    tril = (lax.broadcasted_iota(I32, (ch, ch), 0) >= lax.broadcasted_iota(I32, (ch, ch), 1))
    for g in range(SGU_GROUPS):
        gs = slice(g * ch, (g + 1) * ch)
        wsg = jnp.where(tril, ws_ref[g], 0.0).astype(BF16)
        for c in range(MERGE_TM // ch):
            cs = slice(c * ch, (c + 1) * ch)
            mixed = _dot(wsg, vln_ref[cs, gs]) + bs_ref[g]
            ya_scr[cs, gs] = (u_ref[cs, gs].astype(F32) * mixed).astype(BF16)
    ma = _dot(ya_scr[...], wa_ref[...])
    mb = _dot(yb_ref[...], wb_ref[...])
    merged = gate_ref[:, :d].astype(F32) * ma + gate_ref[:, d:].astype(F32) * mb
    o = _dot(merged.astype(BF16), wo_ref[...])
    o_ref[...] = x_ref[...] + _rms(o, gpost_ref[...])


def _merge(x2, u, vln, yb, gate, w_s, b_s, w_a, w_b, w_o, g_post, layer):
    n, d = x2.shape
    width = u.shape[1]
    one = pl.Buffered(1)
    tok = lambda i: (i, 0)
    const2 = lambda i: (0, 0)
    whole = lambda a: pl.BlockSpec((None,) + a.shape[1:], lambda i: (layer,) + (0,) * (a.ndim - 1),
                                   pipeline_mode=one)
    return pl.pallas_call(
        _merge_body,
        grid=(n // MERGE_TM,),
        in_specs=[
            pl.BlockSpec((MERGE_TM, d), tok),
            pl.BlockSpec((MERGE_TM, width), tok),
            pl.BlockSpec((MERGE_TM, width), tok),
            pl.BlockSpec((MERGE_TM, width), tok),
            pl.BlockSpec((MERGE_TM, 2 * d), tok),
            whole(w_s), whole(b_s), whole(w_a), whole(w_b), whole(w_o),
            pl.BlockSpec((1, d), const2),
        ],
        out_specs=pl.BlockSpec((MERGE_TM, d), tok),
        out_shape=jax.ShapeDtypeStruct((n, d), F32),
        scratch_shapes=[pltpu.VMEM((MERGE_TM, width), BF16)],
        compiler_params=pltpu.CompilerParams(
            dimension_semantics=("parallel",), vmem_limit_bytes=VMEM_LIMIT),
        name="merge",
    )(x2, u, vln, yb, gate, w_s, b_s, w_a, w_b, w_o, g_post.reshape(1, d))


def kernel(x, ffn1_norm_pre, ffn1_norm_post, ffn1_w_in, ffn1_w_out, mix_norm_pre, mix_norm_post, w_in,
           sgu_ln_g, sgu_ln_b, sgu_w_s, sgu_b, rel_bias, w_branch_a, w_branch_b, w_gate, w_out,
           ffn2_norm_pre, ffn2_norm_post, ffn2_w_in, ffn2_w_out):
    b, s, d = x.shape
    depth = w_in.shape[0]
    sgu_w = sgu_ln_g.shape[1]
    att_w = ATT_HEADS * ATT_HEAD_DIM
    idx_w = IDX_HEADS * IDX_DIM
    top_k = min(TOPK_MAX, s // 4)
    assert s % PROJ_TM == 0 and s % ATT_TQ == 0 and (b * s) % FFN_TM == 0
    assert sgu_w == PROJ_BN and att_w == PROJ_BN and idx_w == PROJ_BN and ATT_KU == ATT_TQ

    sizes = (sgu_w, sgu_w, att_w, att_w, att_w, idx_w, IDX_DIM, IDX_HEADS)
    offs = [0]
    for sz in sizes:
        offs.append(offs[-1] + sz)
    col = lambda w, idx: w[:, :, offs[idx]:offs[idx + 1]]
    bf = lambda w: w.astype(BF16)
    w_gate_b = bf(w_gate)
    w_nat = jnp.concatenate([bf(col(w_in, 0)), bf(col(w_in, 1)), bf(col(w_in, 3))], axis=2)
    nat_blocks = (0, 1, 2)
    wt_cat = jnp.swapaxes(jnp.concatenate([bf(col(w_in, 2)), bf(col(w_in, 4)), bf(col(w_in, 5))], axis=2), 1, 2)
    wwt = jnp.swapaxes(bf(col(w_in, 7)), 1, 2)
    w_kidx = bf(col(w_in, 6))
    ffn1_in_b, ffn1_out_b, ffn2_in_b, ffn2_out_b = bf(ffn1_w_in), bf(ffn1_w_out), bf(ffn2_w_in), bf(ffn2_w_out)
    w_a_b, w_b_b, w_o_b = bf(w_branch_a), bf(w_branch_b), bf(w_out)
    b_s = sgu_b.reshape(depth, SGU_GROUPS, SGU_CHUNK, 1)

    bias = _bias_tiles(rel_bias)
    x2 = x.reshape(b * s, d)
    for l in range(depth):
        x2, h2 = _ffn(x2, ffn1_norm_pre[l], ffn1_norm_post[l], ffn1_in_b, ffn1_out_b, l,
                      g_next=mix_norm_pre[l])
        u, vln, k, gate, kidx = _proj_nat(h2, w_nat, w_kidx, w_gate_b, sgu_ln_g[l], sgu_ln_b[l], l,
                                          nat_blocks)
        qt, vt, qit, wit = _proj_t(h2.reshape(b, s, d), wt_cat, wwt, l)
        yb = _attention(kidx.reshape(b, s, IDX_DIM), qit, wit, k.reshape(b, s, att_w), vt, qt, bias, top_k)
        x2 = _merge(x2, u, vln, yb.reshape(b * s, att_w), gate,
                    sgu_w_s, b_s, w_a_b, w_b_b, w_o_b, mix_norm_post[l], l)
        (x2,) = _ffn(x2, ffn2_norm_pre[l], ffn2_norm_post[l], ffn2_in_b, ffn2_out_b, l)
    return x2.reshape(b, s, d)
```

```python
import functools
import math

import jax
import jax.numpy as jnp
from jax import lax
from jax.experimental import pallas as pl
from jax.experimental.pallas import tpu as pltpu

F32 = jnp.float32
BF16 = jnp.bfloat16
I32 = jnp.int32
I16 = jnp.int16

SGU_GROUPS = 8
SGU_CHUNK = 128
ATT_HEADS = 8
ATT_HEAD_DIM = 128
IDX_HEADS = 16
IDX_DIM = 64
TOPK_MAX = 256
NUM_BUCKETS = 32
MAX_DISTANCE = 128
NORM_EPS = 1e-6
LN_EPS = 1e-5

V7X_VMEM_BYTES = 64 * 1024 * 1024
VMEM_LIMIT = V7X_VMEM_BYTES - 8 * 1024 * 1024
SUBLANES = 8
BF16_TILE_ROWS = 16

FFN_TM = 512
FFN_TF = 512
PROJ_TM = 1024
PROJ_BN = 1024
PROJ_CH = 256
MM_AHEAD = 4
MERGE_TM = 256
ATT_TQ = 256
ATT_KU = 256
ACC_ROWS = ATT_HEAD_DIM + BF16_TILE_ROWS
QK_AHEAD = ATT_HEADS
FIXED_MAX_DENOM_LIMIT = 2.0 ** 20

INT_MIN = -(2 ** 31)
I16_MIN, I16_BIAS = -(2 ** 15), 2 ** 15
COUNT_ALL = 2.0 ** 30
MASKED_LOGIT = -1e30
LOG2E = math.log2(math.e)


def _rms(xf, g, scale=1.0):
    ms = jnp.mean(xf * xf, axis=-1, keepdims=True)
    return xf * (scale * lax.rsqrt(ms + NORM_EPS)) * g


def _gelu_tanh(x):
    c = math.sqrt(2.0 / math.pi)
    return x * (0.5 * (1.0 + jnp.tanh(c * (x + 0.044715 * (x * x * x)))))


def _dot(a, b):
    return jnp.dot(a, b, preferred_element_type=F32)


def _dot_nt(a, b):
    return lax.dot_general(a, b, (((1,), (1,)), ((), ())), preferred_element_type=F32)


def _ffn_body(*refs, emit_next):
    if emit_next:
        x_ref, gpre_ref, gpost_ref, gnext_ref, wa_ref, wb_ref, wo_ref, o_ref, hn_ref, h_scr, acc_scr = refs
    else:
        x_ref, gpre_ref, gpost_ref, wa_ref, wb_ref, wo_ref, o_ref, h_scr, acc_scr = refs
    j = pl.program_id(1)

    @pl.when(j == 0)
    def _():
        h_scr[...] = _rms(x_ref[...], gpre_ref[...]).astype(BF16)
        acc_scr[...] = jnp.zeros_like(acc_scr)

    h = h_scr[...]
    a = _dot(h, wa_ref[...])
    b = _dot(h, wb_ref[...])
    g = (a * jax.nn.sigmoid(a) * b).astype(BF16)
    acc_scr[...] += _dot(g, wo_ref[...])

    @pl.when(j == pl.num_programs(1) - 1)
    def _():
        y = x_ref[...] + _rms(acc_scr[...], gpost_ref[...], scale=0.5)
        o_ref[...] = y
        if emit_next:
            hn_ref[...] = _rms(y, gnext_ref[...]).astype(BF16)


def _ffn(x2, g_pre, g_post, w_in, w_out, layer, g_next=None):
    n, d = x2.shape
    d_ff = w_out.shape[1]
    nf = d_ff // FFN_TF
    emit_next = g_next is not None
    vec = pl.BlockSpec((1, d), lambda i, j: (0, 0))
    tok = pl.BlockSpec((FFN_TM, d), lambda i, j: (i, 0))
    gains = [g_pre, g_post] + ([g_next] if emit_next else [])
    out_shape = [jax.ShapeDtypeStruct((n, d), F32)] + ([jax.ShapeDtypeStruct((n, d), BF16)] if emit_next else [])
    return pl.pallas_call(
        functools.partial(_ffn_body, emit_next=emit_next),
        grid=(n // FFN_TM, nf),
        in_specs=[tok] + [vec] * len(gains) + [
            pl.BlockSpec((None, d, FFN_TF), lambda i, j: (layer, 0, j)),
            pl.BlockSpec((None, d, FFN_TF), lambda i, j: (layer, 0, j + nf)),
            pl.BlockSpec((None, FFN_TF, d), lambda i, j: (layer, j, 0)),
        ],
        out_specs=[tok] * len(out_shape),
        out_shape=out_shape,
        scratch_shapes=[pltpu.VMEM((FFN_TM, d), BF16), pltpu.VMEM((FFN_TM, d), F32)],
        compiler_params=pltpu.CompilerParams(
            dimension_semantics=("parallel", "arbitrary"), vmem_limit_bytes=VMEM_LIMIT),
        name="ffn",
    )(x2, *[g.reshape(1, d) for g in gains], w_in, w_in, w_out)


def _chunked(n_chunks, matmul, epilogue):
    z = [matmul(c) for c in range(min(MM_AHEAD, n_chunks))]
    for c in range(n_chunks):
        if c + MM_AHEAD < n_chunks:
            z.append(matmul(c + MM_AHEAD))
        epilogue(c, z[c])
        z[c] = None


def _proj_nat_body(h_ref, w_ref, wki_ref, wg_ref, lng_ref, lnb_ref,
                   u_ref, vln_ref, k_ref, gate_ref, kidx_ref, v_scr):
    j = pl.program_id(1)
    ch = PROJ_CH
    nc = PROJ_BN // ch
    cols = lambda c: slice(c * ch, (c + 1) * ch)
    matmul = lambda c: _dot(h_ref[...], w_ref[:, cols(c)])
    matmul_gate = lambda c: _dot(h_ref[...], wg_ref[:, cols(c)])

    @pl.when(j == 0)
    def _():
        kidx_ref[...] = _dot(h_ref[...], wki_ref[...]).astype(BF16)

        def store_u(c, z):
            u_ref[:, cols(c)] = _gelu_tanh(z).astype(BF16)
        _chunked(nc, matmul, store_u)

    @pl.when(j == 1)
    def _():
        def store_v(c, z):
            v_scr[:, cols(c)] = _gelu_tanh(z)
        _chunked(nc, matmul, store_v)
        v = v_scr[...]
        mu = jnp.mean(v, axis=-1, keepdims=True)
        vc = v - mu
        var = jnp.mean(vc * vc, axis=-1, keepdims=True)
        vln_ref[...] = (vc * lax.rsqrt(var + LN_EPS) * lng_ref[...] + lnb_ref[...]).astype(BF16)

    @pl.when(j == 2)
    def _():
        def store_k(c, z):
            k_ref[:, cols(c)] = z.astype(BF16)
        _chunked(nc, matmul, store_k)

    @pl.when(j >= 3)
    def _():
        def store_gate(c, z):
            gate_ref[:, cols(c)] = jax.nn.sigmoid(z).astype(BF16)
        _chunked(nc, matmul_gate, store_gate)


def _proj_nat(h2, w_in, w_kidx, w_gate, ln_g, ln_b, layer, blocks):
    n, d = h2.shape
    width = PROJ_BN
    ngate = w_gate.shape[2]
    n_in = len(blocks)
    assert n_in == 3
    nj = n_in + ngate // width
    tok = lambda i, j: (i, 0)

    def in_block(i, j):
        blk = blocks[n_in - 1]
        for t in range(n_in - 2, -1, -1):
            blk = jnp.where(j == t, blocks[t], blk)
        return (layer, 0, blk)

    return pl.pallas_call(
        _proj_nat_body,
        grid=(n // PROJ_TM, nj),
        in_specs=[
            pl.BlockSpec((PROJ_TM, d), tok),
            pl.BlockSpec((None, d, width), in_block),
            pl.BlockSpec((None, d, IDX_DIM), lambda i, j: (layer, 0, 0)),
            pl.BlockSpec((None, d, width), lambda i, j: (layer, 0, jnp.maximum(j - n_in, 0))),
            pl.BlockSpec((1, width), lambda i, j: (0, 0)),
            pl.BlockSpec((1, width), lambda i, j: (0, 0)),
        ],
        out_specs=[
            pl.BlockSpec((PROJ_TM, width), tok),
            pl.BlockSpec((PROJ_TM, width), tok),
            pl.BlockSpec((PROJ_TM, width), tok),
            pl.BlockSpec((PROJ_TM, width), lambda i, j: (i, jnp.maximum(j - 3, 0))),
            pl.BlockSpec((PROJ_TM, IDX_DIM), tok),
        ],
        out_shape=[
            jax.ShapeDtypeStruct((n, width), BF16),
            jax.ShapeDtypeStruct((n, width), BF16),
            jax.ShapeDtypeStruct((n, width), BF16),
            jax.ShapeDtypeStruct((n, ngate), BF16),
            jax.ShapeDtypeStruct((n, IDX_DIM), BF16),
        ],
        scratch_shapes=[pltpu.VMEM((PROJ_TM, width), F32)],
        compiler_params=pltpu.CompilerParams(
            dimension_semantics=("parallel", "arbitrary"), vmem_limit_bytes=VMEM_LIMIT),
        name="proj_nat",
    )(h2, w_in, w_kidx, w_gate, ln_g.reshape(1, width), ln_b.reshape(1, width))


def _proj_t_body(h_ref, wt_ref, wwt_ref, qt_ref, vt_ref, qit_ref, wit_ref):
    j = pl.program_id(2)
    ch = PROJ_CH
    nc = PROJ_BN // ch
    rows = lambda c: slice(c * ch, (c + 1) * ch)
    matmul = lambda c: _dot_nt(wt_ref[rows(c), :], h_ref[0])

    @pl.when(j == 0)
    def _():
        wit_ref[0] = _dot_nt(wwt_ref[...], h_ref[0])

        def store_q(c, zt):
            qt_ref[0, rows(c), :] = (zt * (ATT_HEAD_DIM ** -0.5 * LOG2E)).astype(BF16)
        _chunked(nc, matmul, store_q)

    @pl.when(j == 1)
    def _():
        def store_v(c, zt):
            for cc in range(PROJ_TM // ATT_KU):
                vt_ref[0, cc, rows(c), :] = zt[:, cc * ATT_KU:(cc + 1) * ATT_KU].astype(BF16)
        _chunked(nc, matmul, store_v)

    @pl.when(j == 2)
    def _():
        def store_qi(c, zt):
            qit_ref[0, rows(c), :] = zt.astype(BF16)
        _chunked(nc, matmul, store_qi)


def _proj_t(h3, wt_cat, wwt, layer):
    b, s, d = h3.shape
    width = PROJ_BN
    nch = PROJ_TM // ATT_KU
    feat = lambda bi, si, j: (bi, 0, si)
    return pl.pallas_call(
        _proj_t_body,
        grid=(b, s // PROJ_TM, 3),
        in_specs=[
            pl.BlockSpec((1, PROJ_TM, d), lambda bi, si, j: (bi, si, 0)),
            pl.BlockSpec((None, width, d), lambda bi, si, j: (layer, j, 0)),
            pl.BlockSpec((None, IDX_HEADS, d), lambda bi, si, j: (layer, 0, 0)),
        ],
        out_specs=[
            pl.BlockSpec((1, width, PROJ_TM), feat),
            pl.BlockSpec((1, nch, width, ATT_KU), lambda bi, si, j: (bi, si, 0, 0)),
            pl.BlockSpec((1, width, PROJ_TM), feat),
            pl.BlockSpec((1, IDX_HEADS, PROJ_TM), feat),
        ],
        out_shape=[
            jax.ShapeDtypeStruct((b, width, s), BF16),
            jax.ShapeDtypeStruct((b, s // ATT_KU, width, ATT_KU), BF16),
            jax.ShapeDtypeStruct((b, width, s), BF16),
            jax.ShapeDtypeStruct((b, IDX_HEADS, s), F32),
        ],
        compiler_params=pltpu.CompilerParams(
            dimension_semantics=("parallel", "parallel", "arbitrary"), vmem_limit_bytes=VMEM_LIMIT),
        name="proj_t",
    )(h3, wt_cat, wwt)


N_NEAR_UNITS = 2


def _bias_body(rb_ref, o_ref):
    h = pl.program_id(0)
    r = lax.broadcasted_iota(I32, (ATT_KU, ATT_TQ), 0)
    c = lax.broadcasted_iota(I32, (ATT_KU, ATT_TQ), 1)
    max_exact = NUM_BUCKETS // 2
    far = rb_ref[NUM_BUCKETS - 1, h]
    for t in range(N_NEAR_UNITS):
        dist = jnp.maximum(t * ATT_KU + c - r, 0)
        nf = jnp.maximum(dist, 1).astype(F32)
        large = max_exact + (jnp.log(nf / max_exact) / math.log(MAX_DISTANCE / max_exact)
                             * (NUM_BUCKETS - max_exact)).astype(I32)
        large = jnp.minimum(large, NUM_BUCKETS - 1)
        bucket = jnp.where(dist < max_exact, dist, large)
        val = jnp.zeros((ATT_KU, ATT_TQ), F32)
        for bkt in range(NUM_BUCKETS):
            val = jnp.where(bucket == bkt, rb_ref[bkt, h], val)
        o_ref[0, t] = (val - far) * LOG2E


def _bias_tiles(rel_bias):
    assert N_NEAR_UNITS * ATT_KU - (ATT_KU - 1) >= MAX_DISTANCE
    return pl.pallas_call(
        _bias_body,
        grid=(ATT_HEADS,),
        in_specs=[pl.BlockSpec(memory_space=pltpu.SMEM)],
        out_specs=pl.BlockSpec((1, N_NEAR_UNITS, ATT_KU, ATT_TQ), lambda h: (h, 0, 0, 0)),
        out_shape=jax.ShapeDtypeStruct((ATT_HEADS, N_NEAR_UNITS, ATT_KU, ATT_TQ), F32),
        name="bias_tiles",
    )(rel_bias)


def _tree_sum(xs):
    while len(xs) > 1:
        xs = [a + b for a, b in zip(xs[::2], xs[1::2])] + ([xs[-1]] if len(xs) % 2 else [])
    return xs[0]


def _sublane_allmax(x):
    for shift in (4, 2, 1):
        x = jnp.maximum(x, pltpu.roll(x, shift, axis=0))
    return x


def _attn_body(kidx_ref, qit_ref, wit_ref, k_ref, vt_ref, qt_ref, bias_ref, o_ref,
               key_scr, hi_scr, lo_scr, m_scr, acc_scr, save_scr, *, top_k):
    i = pl.program_id(1)
    ku, tq, sl, pk = ATT_KU, ATT_TQ, SUBLANES, BF16_TILE_ROWS
    n_units = i + 1
    q0 = i * tq

    w_all = wit_ref[0] * (IDX_HEADS ** -0.5 * IDX_DIM ** -0.5)
    row = lax.broadcasted_iota(I32, (ku, tq), 0)
    col = lax.broadcasted_iota(I32, (ku, tq), 1)

    def score_unit(u, carry):
        r0 = pl.multiple_of(u * ku, ku)
        kch = kidx_ref[0, pl.ds(r0, ku), :]
        acc = jnp.zeros((ku, tq), F32)
        for h in range(IDX_HEADS):
            d = _dot(kch, qit_ref[0, h * IDX_DIM:(h + 1) * IDX_DIM, :])
            acc = acc + jnp.maximum(d, 0.0) * w_all[h:h + 1, :]
        bits = lax.bitcast_convert_type(acc, I32)
        key = jnp.where(bits < 0, bits ^ jnp.int32(0x7FFFFFFF), bits)
        key = jnp.where(r0 + row <= q0 + col, key, INT_MIN)
        key_scr[pl.ds(r0, ku), :] = key
        hi_scr[pl.ds(r0, ku), :] = lax.shift_right_arithmetic(key, 16).astype(I16)
        lo_scr[pl.ds(r0, ku), :] = ((key & 0xFFFF) - I16_BIAS).astype(I16)
        return carry

    lax.fori_loop(0, n_units, score_unit, 0)

    one_b, zero_b = jnp.ones((), BF16), jnp.zeros((), BF16)

    n_pairs = (n_units + 1) // 2

    @pl.when(n_units % 2 == 1)
    def _():
        r0 = pl.multiple_of(n_units * ku, ku)
        hi_scr[pl.ds(r0, ku), :] = jnp.full((ku, tq), I16_MIN, I16)
        lo_scr[pl.ds(r0, ku), :] = jnp.full((ku, tq), I16_MIN, I16)

    def count_ge(ref, cand):
        cand16 = jnp.broadcast_to(cand, (pk, tq)).astype(I16)

        def body(p, cnt):
            r0 = pl.multiple_of(p * (2 * ku), 2 * ku)
            blk = ref[pl.ds(r0, 2 * ku), :].reshape(2 * ku // pk, pk, tq)
            hit = jnp.where(blk >= cand16[None], one_b, zero_b)
            return cnt + _tree_sum([hit[g] for g in range(2 * ku // pk)]).astype(F32)

        cnt = lax.fori_loop(0, n_pairs, body, jnp.zeros((pk, tq), F32))
        return jnp.sum(cnt, axis=0, keepdims=True)

    def kth_largest(ref, kvec):
        zero = jnp.zeros((1, tq), I32)
        c0 = count_ge(ref, zero)
        ok0 = c0 >= kvec
        state = (jnp.where(ok0, zero, I16_MIN), jnp.where(ok0, c0, COUNT_ALL), jnp.where(ok0, 0.0, c0))

        def descend(t, state):
            prefix, n_ge, n_gt = state
            cand = prefix | jnp.left_shift(jnp.int32(1), 14 - t)
            c = count_ge(ref, cand)
            ok = c >= kvec
            return jnp.where(ok, cand, prefix), jnp.where(ok, c, n_ge), jnp.where(ok, n_gt, c)

        return lax.fori_loop(0, 15, descend, state)

    k_f = jnp.full((1, tq), top_k, F32)
    thr_hi, _, n_above = kth_largest(hi_scr, k_f)
    k_low = k_f - n_above
    thr_hi16 = jnp.broadcast_to(thr_hi, (pk, tq)).astype(I16)

    def keep_low_of_winners(p, carry):
        r0 = pl.multiple_of(p * (2 * ku), 2 * ku)
        hi = hi_scr[pl.ds(r0, 2 * ku), :].reshape(2 * ku // pk, pk, tq)
        lo = lo_scr[pl.ds(r0, 2 * ku), :].reshape(2 * ku // pk, pk, tq)
        lo_scr[pl.ds(r0, 2 * ku), :] = jnp.where(hi == thr_hi16[None], lo,
                                                 jnp.int16(I16_MIN)).reshape(2 * ku, tq)
        return carry

    lax.fori_loop(0, n_pairs, keep_low_of_winners, 0)
    thr_lo, n_ge_lo, n_greater = kth_largest(lo_scr, k_low)
    thr_raw = thr_hi * (1 << 16) + (thr_lo + I16_BIAS)
    thr = jnp.maximum(thr_raw, INT_MIN + 1)

    n_tied_ok = k_low - n_greater
    n_tied = n_ge_lo - n_greater
    has_excess = jnp.logical_and(n_tied > n_tied_ok, thr_raw > INT_MIN)

    @pl.when(jnp.max(jnp.where(has_excess, 1.0, 0.0)) > 0.0)
    def _():
        earlier = (lax.broadcasted_iota(I32, (ku, ku), 0) > lax.broadcasted_iota(I32, (ku, ku), 1))
        earlier = jnp.where(earlier, 1.0, 0.0).astype(BF16)

        def demote(u, seen):
            r0 = pl.multiple_of(u * ku, ku)
            key = key_scr[pl.ds(r0, ku), :]
            tied = key == thr
            rank = _dot(earlier, jnp.where(tied, 1.0, 0.0).astype(BF16)) + seen
            key_scr[pl.ds(r0, ku), :] = jnp.where(jnp.logical_and(tied, rank >= n_tied_ok), INT_MIN, key)
            return seen + jnp.sum(jnp.where(tied, 1.0, 0.0), axis=0, keepdims=True)

        lax.fori_loop(0, n_units, demote, jnp.zeros((1, tq), F32))

    m_scr[...] = jnp.full(m_scr.shape, MASKED_LOGIT, F32)
    acc_scr[...] = jnp.zeros(acc_scr.shape, F32)
    ones_rows = jnp.ones((ACC_ROWS - ATT_HEAD_DIM, ku), BF16)

    def unit(u, near_tile, rescale):
        r0 = pl.multiple_of(u * ku, ku)
        keep = key_scr[pl.ds(r0, ku), :].reshape(ku // sl, sl, tq) >= thr[None]
        mask_add = jnp.where(keep, 0.0, MASKED_LOGIT)

        def qk(h):
            hs = slice(h * ATT_HEAD_DIM, (h + 1) * ATT_HEAD_DIM)
            s = _dot(k_ref[0, pl.ds(r0, ku), hs], qt_ref[0, hs, :])
            if near_tile is not None:
                s = s + bias_ref[h, near_tile]
            return s

        ahead = QK_AHEAD
        pending = [qk(h) for h in range(min(ahead, ATT_HEADS))]
        for h in range(ATT_HEADS):
            hs = slice(h * ATT_HEAD_DIM, (h + 1) * ATT_HEAD_DIM)
            s = pending.pop(0)
            if h + ahead < ATT_HEADS:
                pending.append(qk(h + ahead))
            s = s.reshape(ku // sl, sl, tq) + mask_add
            m_old = m_scr[h]
            if rescale:
                m_new = jnp.maximum(m_old, _sublane_allmax(jnp.max(s, axis=0)))
                alpha = jnp.exp2(m_old - m_new)
                m_scr[h] = m_new
            else:
                m_new = m_old
            p = jnp.exp2(s - m_new[None])
            v_ext = jnp.concatenate([vt_ref[0, u, hs, :], ones_rows], axis=0)
            pv = _dot(v_ext, p.reshape(ku, tq).astype(BF16))
            if rescale:
                acc = acc_scr[h].reshape(ACC_ROWS // sl, sl, tq) * alpha[None]
                acc_scr[h] = pv + acc.reshape(ACC_ROWS, tq)
            else:
                acc_scr[h] = pv + acc_scr[h]

    def run_units(first_far, near_trips, rescale):
        lax.fori_loop(first_far, n_far, lambda u, c: (unit(u, None, rescale), c)[1], 0)
        for t in range(N_NEAR_UNITS - 1, -1, -1):
            lax.fori_loop(0, jnp.minimum(near_trips, jnp.minimum(n_units - t, 1)),
                          lambda _, c, t=t: (unit(i - t, t, rescale), c)[1], 0)

    n_far = jnp.maximum(n_units - N_NEAR_UNITS, 0)
    lead = jnp.minimum(n_far, 1)
    lax.fori_loop(0, lead, lambda u, c: (unit(u, None, True), c)[1], 0)
    save_scr[...] = acc_scr[...]
    run_units(lead, lead, False)
    outgrown = jnp.zeros((1, tq), F32)
    for h in range(ATT_HEADS):
        l = acc_scr[h, ATT_HEAD_DIM:ATT_HEAD_DIM + 1, :]
        outgrown = jnp.maximum(outgrown, jnp.where(l < FIXED_MAX_DENOM_LIMIT, 0.0, 1.0))

    @pl.when(jnp.logical_or(lead == 0, jnp.max(outgrown) > 0.0))
    def _():
        acc_scr[...] = save_scr[...]
        run_units(lead, 1, True)

    for h in range(ATT_HEADS):
        hs = slice(h * ATT_HEAD_DIM, (h + 1) * ATT_HEAD_DIM)
        l = acc_scr[h, ATT_HEAD_DIM:ATT_HEAD_DIM + 1, :]
        o_ref[0, :, hs] = (acc_scr[h, :ATT_HEAD_DIM, :] / l).T.astype(BF16)


def _attention(kidx, qit, wit, k, vt, qt, bias, top_k):
    b, s, width = k.shape
    one = pl.Buffered(1)
    return pl.pallas_call(
        functools.partial(_attn_body, top_k=top_k),
        grid=(b, s // ATT_TQ),
        in_specs=[
            pl.BlockSpec((1, s, IDX_DIM), lambda bi, i: (bi, 0, 0), pipeline_mode=one),
            pl.BlockSpec((1, IDX_HEADS * IDX_DIM, ATT_TQ), lambda bi, i: (bi, 0, i)),
            pl.BlockSpec((1, IDX_HEADS, ATT_TQ), lambda bi, i: (bi, 0, i)),
            pl.BlockSpec((1, s, width), lambda bi, i: (bi, 0, 0), pipeline_mode=one),
            pl.BlockSpec((1, s // ATT_KU, width, ATT_KU), lambda bi, i: (bi, 0, 0, 0), pipeline_mode=one),
            pl.BlockSpec((1, width, ATT_TQ), lambda bi, i: (bi, 0, i)),
            pl.BlockSpec(bias.shape, lambda bi, i: (0, 0, 0, 0), pipeline_mode=one),
        ],
        out_specs=pl.BlockSpec((1, ATT_TQ, width), lambda bi, i: (bi, i, 0)),
        out_shape=jax.ShapeDtypeStruct((b, s, width), BF16),
        scratch_shapes=[
            pltpu.VMEM((s, ATT_TQ), I32),
            pltpu.VMEM((s, ATT_TQ), I16),
            pltpu.VMEM((s, ATT_TQ), I16),
            pltpu.VMEM((ATT_HEADS, SUBLANES, ATT_TQ), F32),
            pltpu.VMEM((ATT_HEADS, ACC_ROWS, ATT_TQ), F32),
            pltpu.VMEM((ATT_HEADS, ACC_ROWS, ATT_TQ), F32),
        ],
        compiler_params=pltpu.CompilerParams(
            dimension_semantics=("parallel", "arbitrary"), vmem_limit_bytes=VMEM_LIMIT),
        name="dsa_attention",
    )(kidx, qit, wit, k, vt, qt, bias)


def _merge_body(x_ref, u_ref, vln_ref, yb_ref, gate_ref, ws_ref, bs_ref, wa_ref, wb_ref, wo_ref,
                gpost_ref, o_ref, ya_scr):
    ch = SGU_CHUNK
    d = x_ref.shape[1]
    tril = (lax.broadcasted_iota(I32, (ch, ch), 0) >= lax.broadcasted_iota(I32, (ch, ch), 1))
    for g in range(SGU_GROUPS):
        gs = slice(g * ch, (g + 1) * ch)
        wsg = jnp.where(tril, ws_ref[g], 0.0).astype(BF16)
        for c in range(MERGE_TM // ch):
            cs = slice(c * ch, (c + 1) * ch)
            mixed = _dot(wsg, vln_ref[cs, gs]) + bs_ref[g]
            ya_scr[cs, gs] = (u_ref[cs, gs].astype(F32) * mixed).astype(BF16)
    ma = _dot(ya_scr[...], wa_ref[...])
    mb = _dot(yb_ref[...], wb_ref[...])
    merged = gate_ref[:, :d].astype(F32) * ma + gate_ref[:, d:].astype(F32) * mb
    o = _dot(merged.astype(BF16), wo_ref[...])
    o_ref[...] = x_ref[...] + _rms(o, gpost_ref[...])


def _merge(x2, u, vln, yb, gate, w_s, b_s, w_a, w_b, w_o, g_post, layer):
    n, d = x2.shape
    width = u.shape[1]
    one = pl.Buffered(1)
    tok = lambda i: (i, 0)
    const2 = lambda i: (0, 0)
    whole = lambda a: pl.BlockSpec((None,) + a.shape[1:], lambda i: (layer,) + (0,) * (a.ndim - 1),
                                   pipeline_mode=one)
    return pl.pallas_call(
        _merge_body,
        grid=(n // MERGE_TM,),
        in_specs=[
            pl.BlockSpec((MERGE_TM, d), tok),
            pl.BlockSpec((MERGE_TM, width), tok),
            pl.BlockSpec((MERGE_TM, width), tok),
            pl.BlockSpec((MERGE_TM, width), tok),
            pl.BlockSpec((MERGE_TM, 2 * d), tok),
            whole(w_s), whole(b_s), whole(w_a), whole(w_b), whole(w_o),
            pl.BlockSpec((1, d), const2),
        ],
        out_specs=pl.BlockSpec((MERGE_TM, d), tok),
        out_shape=jax.ShapeDtypeStruct((n, d), F32),
        scratch_shapes=[pltpu.VMEM((MERGE_TM, width), BF16)],
        compiler_params=pltpu.CompilerParams(
            dimension_semantics=("parallel",), vmem_limit_bytes=VMEM_LIMIT),
        name="merge",
    )(x2, u, vln, yb, gate, w_s, b_s, w_a, w_b, w_o, g_post.reshape(1, d))


def kernel(x, ffn1_norm_pre, ffn1_norm_post, ffn1_w_in, ffn1_w_out, mix_norm_pre, mix_norm_post, w_in,
           sgu_ln_g, sgu_ln_b, sgu_w_s, sgu_b, rel_bias, w_branch_a, w_branch_b, w_gate, w_out,
           ffn2_norm_pre, ffn2_norm_post, ffn2_w_in, ffn2_w_out):
    b, s, d = x.shape
    depth = w_in.shape[0]
    sgu_w = sgu_ln_g.shape[1]
    att_w = ATT_HEADS * ATT_HEAD_DIM
    idx_w = IDX_HEADS * IDX_DIM
    top_k = min(TOPK_MAX, s // 4)
    assert s % PROJ_TM == 0 and s % ATT_TQ == 0 and (b * s) % FFN_TM == 0
    assert sgu_w == PROJ_BN and att_w == PROJ_BN and idx_w == PROJ_BN and ATT_KU == ATT_TQ

    sizes = (sgu_w, sgu_w, att_w, att_w, att_w, idx_w, IDX_DIM, IDX_HEADS)
    offs = [0]
    for sz in sizes:
        offs.append(offs[-1] + sz)
    col = lambda w, idx: w[:, :, offs[idx]:offs[idx + 1]]
    bf = lambda w: w.astype(BF16)
    w_gate_b = bf(w_gate)
    w_nat = jnp.concatenate([bf(col(w_in, 0)), bf(col(w_in, 1)), bf(col(w_in, 3))], axis=2)
    nat_blocks = (0, 1, 2)
    wt_cat = jnp.swapaxes(jnp.concatenate([bf(col(w_in, 2)), bf(col(w_in, 4)), bf(col(w_in, 5))], axis=2), 1, 2)
    wwt = jnp.swapaxes(bf(col(w_in, 7)), 1, 2)
    w_kidx = bf(col(w_in, 6))
    ffn1_in_b, ffn1_out_b, ffn2_in_b, ffn2_out_b = bf(ffn1_w_in), bf(ffn1_w_out), bf(ffn2_w_in), bf(ffn2_w_out)
    w_a_b, w_b_b, w_o_b = bf(w_branch_a), bf(w_branch_b), bf(w_out)
    b_s = sgu_b.reshape(depth, SGU_GROUPS, SGU_CHUNK, 1)

    bias = _bias_tiles(rel_bias)
    x2 = x.reshape(b * s, d)
    for l in range(depth):
        x2, h2 = _ffn(x2, ffn1_norm_pre[l], ffn1_norm_post[l], ffn1_in_b, ffn1_out_b, l,
                      g_next=mix_norm_pre[l])
        u, vln, k, gate, kidx = _proj_nat(h2, w_nat, w_kidx, w_gate_b, sgu_ln_g[l], sgu_ln_b[l], l,
                                          nat_blocks)
        qt, vt, qit, wit = _proj_t(h2.reshape(b, s, d), wt_cat, wwt, l)
        yb = _attention(kidx.reshape(b, s, IDX_DIM), qit, wit, k.reshape(b, s, att_w), vt, qt, bias, top_k)
        x2 = _merge(x2, u, vln, yb.reshape(b * s, att_w), gate,
                    sgu_w_s, b_s, w_a_b, w_b_b, w_o_b, mix_norm_post[l], l)
        (x2,) = _ffn(x2, ffn2_norm_pre[l], ffn2_norm_post[l], ffn2_in_b, ffn2_out_b, l)
    return x2.reshape(b, s, d)
```

```python
import functools
import math

import jax
import jax.numpy as jnp
from jax import lax
from jax.experimental import pallas as pl
from jax.experimental.pallas import tpu as pltpu

F32 = jnp.float32
BF16 = jnp.bfloat16
I32 = jnp.int32
I16 = jnp.int16

SGU_GROUPS = 8
SGU_CHUNK = 128
ATT_HEADS = 8
ATT_HEAD_DIM = 128
IDX_HEADS = 16
IDX_DIM = 64
TOPK_MAX = 256
NUM_BUCKETS = 32
MAX_DISTANCE = 128
NORM_EPS = 1e-6
LN_EPS = 1e-5

V7X_VMEM_BYTES = 64 * 1024 * 1024
VMEM_LIMIT = V7X_VMEM_BYTES - 8 * 1024 * 1024
SUBLANES = 8
BF16_TILE_ROWS = 16

FFN_TM = 512
FFN_TF = 512
PROJ_TM = 1024
PROJ_BN = 1024
PROJ_CH = 256
MM_AHEAD = 4
MERGE_TM = 256
ATT_TQ = 256
ATT_KU = 256
ACC_ROWS = ATT_HEAD_DIM + BF16_TILE_ROWS
QK_AHEAD = ATT_HEADS
FIXED_MAX_DENOM_LIMIT = 2.0 ** 20

INT_MIN = -(2 ** 31)
I16_MIN, I16_BIAS = -(2 ** 15), 2 ** 15
COUNT_ALL = 2.0 ** 30
MASKED_LOGIT = -1e30
LOG2E = math.log2(math.e)


def _rms(xf, g, scale=1.0):
    ms = jnp.mean(xf * xf, axis=-1, keepdims=True)
    return xf * (scale * lax.rsqrt(ms + NORM_EPS)) * g


def _gelu_tanh(x):
    c = math.sqrt(2.0 / math.pi)
    return x * (0.5 * (1.0 + jnp.tanh(c * (x + 0.044715 * (x * x * x)))))


def _dot(a, b):
    return jnp.dot(a, b, preferred_element_type=F32)


def _dot_nt(a, b):
    return lax.dot_general(a, b, (((1,), (1,)), ((), ())), preferred_element_type=F32)


def _ffn_body(*refs, emit_next):
    if emit_next:
        x_ref, gpre_ref, gpost_ref, gnext_ref, wa_ref, wb_ref, wo_ref, o_ref, hn_ref, h_scr, acc_scr = refs
    else:
        x_ref, gpre_ref, gpost_ref, wa_ref, wb_ref, wo_ref, o_ref, h_scr, acc_scr = refs
    j = pl.program_id(1)

    @pl.when(j == 0)
    def _():
        h_scr[...] = _rms(x_ref[...], gpre_ref[...]).astype(BF16)
        acc_scr[...] = jnp.zeros_like(acc_scr)

    h = h_scr[...]
    a = _dot(h, wa_ref[...])
    b = _dot(h, wb_ref[...])
    g = (a * jax.nn.sigmoid(a) * b).astype(BF16)
    acc_scr[...] += _dot(g, wo_ref[...])

    @pl.when(j == pl.num_programs(1) - 1)
    def _():
        y = x_ref[...] + _rms(acc_scr[...], gpost_ref[...], scale=0.5)
        o_ref[...] = y
        if emit_next:
            hn_ref[...] = _rms(y, gnext_ref[...]).astype(BF16)


def _ffn(x2, g_pre, g_post, w_in, w_out, layer, g_next=None):
    n, d = x2.shape
    d_ff = w_out.shape[1]
    nf = d_ff // FFN_TF
    emit_next = g_next is not None
    vec = pl.BlockSpec((1, d), lambda i, j: (0, 0))
    tok = pl.BlockSpec((FFN_TM, d), lambda i, j: (i, 0))
    gains = [g_pre, g_post] + ([g_next] if emit_next else [])
    out_shape = [jax.ShapeDtypeStruct((n, d), F32)] + ([jax.ShapeDtypeStruct((n, d), BF16)] if emit_next else [])
    return pl.pallas_call(
        functools.partial(_ffn_body, emit_next=emit_next),
        grid=(n // FFN_TM, nf),
        in_specs=[tok] + [vec] * len(gains) + [
            (pl.BlockSpec((None, None, d, FFN_TF), lambda i, j: (layer, j, 0, 0)) if w_in.ndim == 4 else
             pl.BlockSpec((None, d, FFN_TF), lambda i, j: (layer, 0, j))),
            (pl.BlockSpec((None, None, d, FFN_TF), lambda i, j: (layer, j + nf, 0, 0)) if w_in.ndim == 4 else
             pl.BlockSpec((None, d, FFN_TF), lambda i, j: (layer, 0, j + nf))),
            pl.BlockSpec((None, FFN_TF, d), lambda i, j: (layer, j, 0)),
        ],
        out_specs=[tok] * len(out_shape),
        out_shape=out_shape,
        scratch_shapes=[pltpu.VMEM((FFN_TM, d), BF16), pltpu.VMEM((FFN_TM, d), F32)],
        compiler_params=pltpu.CompilerParams(
            dimension_semantics=("parallel", "arbitrary"), vmem_limit_bytes=VMEM_LIMIT),
        name="ffn",
    )(x2, *[g.reshape(1, d) for g in gains], w_in, w_in, w_out)


def _chunked(n_chunks, matmul, epilogue):
    z = [matmul(c) for c in range(min(MM_AHEAD, n_chunks))]
    for c in range(n_chunks):
        if c + MM_AHEAD < n_chunks:
            z.append(matmul(c + MM_AHEAD))
        epilogue(c, z[c])
        z[c] = None


def _proj_nat_body(h_ref, w_ref, wki_ref, wg_ref, lng_ref, lnb_ref,
                   u_ref, vln_ref, k_ref, gate_ref, kidx_ref, v_scr):
    j = pl.program_id(1)
    ch = PROJ_CH
    nc = PROJ_BN // ch
    cols = lambda c: slice(c * ch, (c + 1) * ch)
    matmul = lambda c: _dot(h_ref[...], w_ref[:, cols(c)])
    matmul_gate = lambda c: _dot(h_ref[...], wg_ref[:, cols(c)])

    @pl.when(j == 0)
    def _():
        kidx_ref[...] = _dot(h_ref[...], wki_ref[...]).astype(BF16)

        def store_u(c, z):
            u_ref[:, cols(c)] = _gelu_tanh(z).astype(BF16)
        _chunked(nc, matmul, store_u)

    @pl.when(j == 1)
    def _():
        def store_v(c, z):
            v_scr[:, cols(c)] = _gelu_tanh(z)
        _chunked(nc, matmul, store_v)
        v = v_scr[...]
        mu = jnp.mean(v, axis=-1, keepdims=True)
        vc = v - mu
        var = jnp.mean(vc * vc, axis=-1, keepdims=True)
        vln_ref[...] = (vc * lax.rsqrt(var + LN_EPS) * lng_ref[...] + lnb_ref[...]).astype(BF16)

    @pl.when(j == 2)
    def _():
        def store_k(c, z):
            k_ref[:, cols(c)] = z.astype(BF16)
        _chunked(nc, matmul, store_k)

    @pl.when(j >= 3)
    def _():
        def store_gate(c, z):
            gate_ref[:, cols(c)] = jax.nn.sigmoid(z).astype(BF16)
        _chunked(nc, matmul_gate, store_gate)


def _proj_nat(h2, w_in, w_kidx, w_gate, ln_g, ln_b, layer, blocks):
    n, d = h2.shape
    width = PROJ_BN
    ngate = w_gate.shape[2]
    n_in = len(blocks)
    assert n_in == 3
    nj = n_in + ngate // width
    tok = lambda i, j: (i, 0)

    def in_block(i, j):
        blk = blocks[n_in - 1]
        for t in range(n_in - 2, -1, -1):
            blk = jnp.where(j == t, blocks[t], blk)
        return (layer, 0, blk)

    return pl.pallas_call(
        _proj_nat_body,
        grid=(n // PROJ_TM, nj),
        in_specs=[
            pl.BlockSpec((PROJ_TM, d), tok),
            pl.BlockSpec((None, d, width), in_block),
            pl.BlockSpec((None, d, IDX_DIM), lambda i, j: (layer, 0, 0)),
            pl.BlockSpec((None, d, width), lambda i, j: (layer, 0, jnp.maximum(j - n_in, 0))),
            pl.BlockSpec((1, width), lambda i, j: (0, 0)),
            pl.BlockSpec((1, width), lambda i, j: (0, 0)),
        ],
        out_specs=[
            pl.BlockSpec((PROJ_TM, width), tok),
            pl.BlockSpec((PROJ_TM, width), tok),
            pl.BlockSpec((PROJ_TM, width), tok),
            pl.BlockSpec((PROJ_TM, width), lambda i, j: (i, jnp.maximum(j - 3, 0))),
            pl.BlockSpec((PROJ_TM, IDX_DIM), tok),
        ],
        out_shape=[
            jax.ShapeDtypeStruct((n, width), BF16),
            jax.ShapeDtypeStruct((n, width), BF16),
            jax.ShapeDtypeStruct((n, width), BF16),
            jax.ShapeDtypeStruct((n, ngate), BF16),
            jax.ShapeDtypeStruct((n, IDX_DIM), BF16),
        ],
        scratch_shapes=[pltpu.VMEM((PROJ_TM, width), F32)],
        compiler_params=pltpu.CompilerParams(
            dimension_semantics=("parallel", "arbitrary"), vmem_limit_bytes=VMEM_LIMIT),
        name="proj_nat",
    )(h2, w_in, w_kidx, w_gate, ln_g.reshape(1, width), ln_b.reshape(1, width))


def _proj_t_body(h_ref, wt_ref, wwt_ref, qt_ref, vt_ref, qit_ref, wit_ref):
    j = pl.program_id(2)
    ch = PROJ_CH
    nc = PROJ_BN // ch
    rows = lambda c: slice(c * ch, (c + 1) * ch)
    matmul = lambda c: _dot_nt(wt_ref[rows(c), :], h_ref[0])

    @pl.when(j == 0)
    def _():
        wit_ref[0] = _dot_nt(wwt_ref[...], h_ref[0])

        def store_q(c, zt):
            qt_ref[0, rows(c), :] = (zt * (ATT_HEAD_DIM ** -0.5 * LOG2E)).astype(BF16)
        _chunked(nc, matmul, store_q)

    @pl.when(j == 1)
    def _():
        def store_v(c, zt):
            for cc in range(PROJ_TM // ATT_KU):
                vt_ref[0, cc, rows(c), :] = zt[:, cc * ATT_KU:(cc + 1) * ATT_KU].astype(BF16)
        _chunked(nc, matmul, store_v)

    @pl.when(j == 2)
    def _():
        def store_qi(c, zt):
            qit_ref[0, rows(c), :] = zt.astype(BF16)
        _chunked(nc, matmul, store_qi)


def _proj_t(h3, wt_cat, wwt, layer):
    b, s, d = h3.shape
    width = PROJ_BN
    nch = PROJ_TM // ATT_KU
    feat = lambda bi, si, j: (bi, 0, si)
    return pl.pallas_call(
        _proj_t_body,
        grid=(b, s // PROJ_TM, 3),
        in_specs=[
            pl.BlockSpec((1, PROJ_TM, d), lambda bi, si, j: (bi, si, 0)),
            pl.BlockSpec((None, width, d), lambda bi, si, j: (layer, j, 0)),
            pl.BlockSpec((None, IDX_HEADS, d), lambda bi, si, j: (layer, 0, 0)),
        ],
        out_specs=[
            pl.BlockSpec((1, width, PROJ_TM), feat),
            pl.BlockSpec((1, nch, width, ATT_KU), lambda bi, si, j: (bi, si, 0, 0)),
            pl.BlockSpec((1, width, PROJ_TM), feat),
            pl.BlockSpec((1, IDX_HEADS, PROJ_TM), feat),
        ],
        out_shape=[
            jax.ShapeDtypeStruct((b, width, s), BF16),
            jax.ShapeDtypeStruct((b, s // ATT_KU, width, ATT_KU), BF16),
            jax.ShapeDtypeStruct((b, width, s), BF16),
            jax.ShapeDtypeStruct((b, IDX_HEADS, s), F32),
        ],
        compiler_params=pltpu.CompilerParams(
            dimension_semantics=("parallel", "parallel", "arbitrary"), vmem_limit_bytes=VMEM_LIMIT),
        name="proj_t",
    )(h3, wt_cat, wwt)


N_NEAR_UNITS = 2


def _bias_body(rb_ref, o_ref):
    h = pl.program_id(0)
    r = lax.broadcasted_iota(I32, (ATT_KU, ATT_TQ), 0)
    c = lax.broadcasted_iota(I32, (ATT_KU, ATT_TQ), 1)
    max_exact = NUM_BUCKETS // 2
    far = rb_ref[NUM_BUCKETS - 1, h]
    for t in range(N_NEAR_UNITS):
        dist = jnp.maximum(t * ATT_KU + c - r, 0)
        nf = jnp.maximum(dist, 1).astype(F32)
        large = max_exact + (jnp.log(nf / max_exact) / math.log(MAX_DISTANCE / max_exact)
                             * (NUM_BUCKETS - max_exact)).astype(I32)
        large = jnp.minimum(large, NUM_BUCKETS - 1)
        bucket = jnp.where(dist < max_exact, dist, large)
        val = jnp.zeros((ATT_KU, ATT_TQ), F32)
        for bkt in range(NUM_BUCKETS):
            val = jnp.where(bucket == bkt, rb_ref[bkt, h], val)
        o_ref[0, t] = (val - far) * LOG2E


def _bias_tiles(rel_bias):
    assert N_NEAR_UNITS * ATT_KU - (ATT_KU - 1) >= MAX_DISTANCE
    return pl.pallas_call(
        _bias_body,
        grid=(ATT_HEADS,),
        in_specs=[pl.BlockSpec(memory_space=pltpu.SMEM)],
        out_specs=pl.BlockSpec((1, N_NEAR_UNITS, ATT_KU, ATT_TQ), lambda h: (h, 0, 0, 0)),
        out_shape=jax.ShapeDtypeStruct((ATT_HEADS, N_NEAR_UNITS, ATT_KU, ATT_TQ), F32),
        name="bias_tiles",
    )(rel_bias)


def _tree_sum(xs):
    while len(xs) > 1:
        xs = [a + b for a, b in zip(xs[::2], xs[1::2])] + ([xs[-1]] if len(xs) % 2 else [])
    return xs[0]


def _sublane_allmax(x):
    for shift in (4, 2, 1):
        x = jnp.maximum(x, pltpu.roll(x, shift, axis=0))
    return x


def _attn_body(kidx_ref, qit_ref, wit_ref, k_ref, vt_ref, qt_ref, bias_ref, o_ref,
               key_scr, hi_scr, lo_scr, m_scr, acc_scr, save_scr, *, top_k):
    i = pl.program_id(1)
    ku, tq, sl, pk = ATT_KU, ATT_TQ, SUBLANES, BF16_TILE_ROWS
    n_units = i + 1
    q0 = i * tq

    w_all = wit_ref[0] * (IDX_HEADS ** -0.5 * IDX_DIM ** -0.5)
    row = lax.broadcasted_iota(I32, (ku, tq), 0)
    col = lax.broadcasted_iota(I32, (ku, tq), 1)

    def score_unit(u, carry):
        r0 = pl.multiple_of(u * ku, ku)
        kch = kidx_ref[0, pl.ds(r0, ku), :]
        acc = jnp.zeros((ku, tq), F32)
        for h in range(IDX_HEADS):
            d = _dot(kch, qit_ref[0, h * IDX_DIM:(h + 1) * IDX_DIM, :])
            acc = acc + jnp.maximum(d, 0.0) * w_all[h:h + 1, :]
        bits = lax.bitcast_convert_type(acc, I32)
        key = jnp.where(bits < 0, bits ^ jnp.int32(0x7FFFFFFF), bits)
        key = jnp.where(r0 + row <= q0 + col, key, INT_MIN)
        key_scr[pl.ds(r0, ku), :] = key
        hi_scr[pl.ds(r0, ku), :] = lax.shift_right_arithmetic(key, 16).astype(I16)
        lo_scr[pl.ds(r0, ku), :] = ((key & 0xFFFF) - I16_BIAS).astype(I16)
        return carry

    lax.fori_loop(0, n_units, score_unit, 0)

    one_b, zero_b = jnp.ones((), BF16), jnp.zeros((), BF16)

    n_pairs = (n_units + 1) // 2

    @pl.when(n_units % 2 == 1)
    def _():
        r0 = pl.multiple_of(n_units * ku, ku)
        hi_scr[pl.ds(r0, ku), :] = jnp.full((ku, tq), I16_MIN, I16)
        lo_scr[pl.ds(r0, ku), :] = jnp.full((ku, tq), I16_MIN, I16)

    def count_ge(ref, cand):
        cand16 = jnp.broadcast_to(cand, (pk, tq)).astype(I16)

        def body(p, cnt):
            r0 = pl.multiple_of(p * (2 * ku), 2 * ku)
            blk = ref[pl.ds(r0, 2 * ku), :].reshape(2 * ku // pk, pk, tq)
            hit = jnp.where(blk >= cand16[None], one_b, zero_b)
            return cnt + _tree_sum([hit[g] for g in range(2 * ku // pk)]).astype(F32)

        cnt = lax.fori_loop(0, n_pairs, body, jnp.zeros((pk, tq), F32))
        return jnp.sum(cnt, axis=0, keepdims=True)

    def kth_largest(ref, kvec):
        zero = jnp.zeros((1, tq), I32)
        c0 = count_ge(ref, zero)
        ok0 = c0 >= kvec
        state = (jnp.where(ok0, zero, I16_MIN), jnp.where(ok0, c0, COUNT_ALL), jnp.where(ok0, 0.0, c0))

        def descend(t, state):
            prefix, n_ge, n_gt = state
            cand = prefix | jnp.left_shift(jnp.int32(1), 14 - t)
            c = count_ge(ref, cand)
            ok = c >= kvec
            return jnp.where(ok, cand, prefix), jnp.where(ok, c, n_ge), jnp.where(ok, n_gt, c)

        return lax.fori_loop(0, 15, descend, state)

    k_f = jnp.full((1, tq), top_k, F32)
    thr_hi, _, n_above = kth_largest(hi_scr, k_f)
    k_low = k_f - n_above
    thr_hi16 = jnp.broadcast_to(thr_hi, (pk, tq)).astype(I16)

    def keep_low_of_winners(p, carry):
        r0 = pl.multiple_of(p * (2 * ku), 2 * ku)
        hi = hi_scr[pl.ds(r0, 2 * ku), :].reshape(2 * ku // pk, pk, tq)
        lo = lo_scr[pl.ds(r0, 2 * ku), :].reshape(2 * ku // pk, pk, tq)
        lo_scr[pl.ds(r0, 2 * ku), :] = jnp.where(hi == thr_hi16[None], lo,
                                                 jnp.int16(I16_MIN)).reshape(2 * ku, tq)
        return carry

    lax.fori_loop(0, n_pairs, keep_low_of_winners, 0)
    thr_lo, n_ge_lo, n_greater = kth_largest(lo_scr, k_low)
    thr_raw = thr_hi * (1 << 16) + (thr_lo + I16_BIAS)
    thr = jnp.maximum(thr_raw, INT_MIN + 1)

    n_tied_ok = k_low - n_greater
    n_tied = n_ge_lo - n_greater
    has_excess = jnp.logical_and(n_tied > n_tied_ok, thr_raw > INT_MIN)

    @pl.when(jnp.max(jnp.where(has_excess, 1.0, 0.0)) > 0.0)
    def _():
        earlier = (lax.broadcasted_iota(I32, (ku, ku), 0) > lax.broadcasted_iota(I32, (ku, ku), 1))
        earlier = jnp.where(earlier, 1.0, 0.0).astype(BF16)

        def demote(u, seen):
            r0 = pl.multiple_of(u * ku, ku)
            key = key_scr[pl.ds(r0, ku), :]
            tied = key == thr
            rank = _dot(earlier, jnp.where(tied, 1.0, 0.0).astype(BF16)) + seen
            key_scr[pl.ds(r0, ku), :] = jnp.where(jnp.logical_and(tied, rank >= n_tied_ok), INT_MIN, key)
            return seen + jnp.sum(jnp.where(tied, 1.0, 0.0), axis=0, keepdims=True)

        lax.fori_loop(0, n_units, demote, jnp.zeros((1, tq), F32))

    m_scr[...] = jnp.full(m_scr.shape, MASKED_LOGIT, F32)
    acc_scr[...] = jnp.zeros(acc_scr.shape, F32)
    ones_rows = jnp.ones((ACC_ROWS - ATT_HEAD_DIM, ku), BF16)

    def unit(u, near_tile, rescale):
        r0 = pl.multiple_of(u * ku, ku)
        keep = key_scr[pl.ds(r0, ku), :].reshape(ku // sl, sl, tq) >= thr[None]
        mask_add = jnp.where(keep, 0.0, MASKED_LOGIT)

        def qk(h):
            hs = slice(h * ATT_HEAD_DIM, (h + 1) * ATT_HEAD_DIM)
            s = _dot(k_ref[0, pl.ds(r0, ku), hs], qt_ref[0, hs, :])
            if near_tile is not None:
                s = s + bias_ref[h, near_tile]
            return s

        ahead = QK_AHEAD
        pending = [qk(h) for h in range(min(ahead, ATT_HEADS))]
        for h in range(ATT_HEADS):
            hs = slice(h * ATT_HEAD_DIM, (h + 1) * ATT_HEAD_DIM)
            s = pending.pop(0)
            if h + ahead < ATT_HEADS:
                pending.append(qk(h + ahead))
            s = s.reshape(ku // sl, sl, tq) + mask_add
            m_old = m_scr[h]
            if rescale:
                m_new = jnp.maximum(m_old, _sublane_allmax(jnp.max(s, axis=0)))
                alpha = jnp.exp2(m_old - m_new)
                m_scr[h] = m_new
            else:
                m_new = m_old
            p = jnp.exp2(s - m_new[None])
            v_ext = jnp.concatenate([vt_ref[0, u, hs, :], ones_rows], axis=0)
            pv = _dot(v_ext, p.reshape(ku, tq).astype(BF16))
            if rescale:
                acc = acc_scr[h].reshape(ACC_ROWS // sl, sl, tq) * alpha[None]
                acc_scr[h] = pv + acc.reshape(ACC_ROWS, tq)
            else:
                acc_scr[h] = pv + acc_scr[h]

    def run_units(first_far, near_trips, rescale):
        lax.fori_loop(first_far, n_far, lambda u, c: (unit(u, None, rescale), c)[1], 0)
        for t in range(N_NEAR_UNITS - 1, -1, -1):
            lax.fori_loop(0, jnp.minimum(near_trips, jnp.minimum(n_units - t, 1)),
                          lambda _, c, t=t: (unit(i - t, t, rescale), c)[1], 0)

    n_far = jnp.maximum(n_units - N_NEAR_UNITS, 0)
    lead = jnp.minimum(n_far, 1)
    lax.fori_loop(0, lead, lambda u, c: (unit(u, None, True), c)[1], 0)
    save_scr[...] = acc_scr[...]
    run_units(lead, lead, False)
    outgrown = jnp.zeros((1, tq), F32)
    for h in range(ATT_HEADS):
        l = acc_scr[h, ATT_HEAD_DIM:ATT_HEAD_DIM + 1, :]
        outgrown = jnp.maximum(outgrown, jnp.where(l < FIXED_MAX_DENOM_LIMIT, 0.0, 1.0))

    @pl.when(jnp.logical_or(lead == 0, jnp.max(outgrown) > 0.0))
    def _():
        acc_scr[...] = save_scr[...]
        run_units(lead, 1, True)

    for h in range(ATT_HEADS):
        hs = slice(h * ATT_HEAD_DIM, (h + 1) * ATT_HEAD_DIM)
        l = acc_scr[h, ATT_HEAD_DIM:ATT_HEAD_DIM + 1, :]
        o_ref[0, :, hs] = (acc_scr[h, :ATT_HEAD_DIM, :] / l).T.astype(BF16)


def _attention(kidx, qit, wit, k, vt, qt, bias, top_k):
    b, s, width = k.shape
    one = pl.Buffered(1)
    return pl.pallas_call(
        functools.partial(_attn_body, top_k=top_k),
        grid=(b, s // ATT_TQ),
        in_specs=[
            pl.BlockSpec((1, s, IDX_DIM), lambda bi, i: (bi, 0, 0), pipeline_mode=one),
            pl.BlockSpec((1, IDX_HEADS * IDX_DIM, ATT_TQ), lambda bi, i: (bi, 0, i)),
            pl.BlockSpec((1, IDX_HEADS, ATT_TQ), lambda bi, i: (bi, 0, i)),
            pl.BlockSpec((1, s, width), lambda bi, i: (bi, 0, 0), pipeline_mode=one),
            pl.BlockSpec((1, s // ATT_KU, width, ATT_KU), lambda bi, i: (bi, 0, 0, 0), pipeline_mode=one),
            pl.BlockSpec((1, width, ATT_TQ), lambda bi, i: (bi, 0, i)),
            pl.BlockSpec(bias.shape, lambda bi, i: (0, 0, 0, 0), pipeline_mode=one),
        ],
        out_specs=pl.BlockSpec((1, ATT_TQ, width), lambda bi, i: (bi, i, 0)),
        out_shape=jax.ShapeDtypeStruct((b, s, width), BF16),
        scratch_shapes=[
            pltpu.VMEM((s, ATT_TQ), I32),
            pltpu.VMEM((s, ATT_TQ), I16),
            pltpu.VMEM((s, ATT_TQ), I16),
            pltpu.VMEM((ATT_HEADS, SUBLANES, ATT_TQ), F32),
            pltpu.VMEM((ATT_HEADS, ACC_ROWS, ATT_TQ), F32),
            pltpu.VMEM((ATT_HEADS, ACC_ROWS, ATT_TQ), F32),
        ],
        compiler_params=pltpu.CompilerParams(
            dimension_semantics=("parallel", "arbitrary"), vmem_limit_bytes=VMEM_LIMIT),
        name="dsa_attention",
    )(kidx, qit, wit, k, vt, qt, bias)


def _merge_body(x_ref, u_ref, vln_ref, yb_ref, gate_ref, ws_ref, bs_ref, wa_ref, wb_ref, wo_ref,
                gpost_ref, o_ref, ya_scr):
    ch = SGU_CHUNK
    d = x_ref.shape[1]
    tril = (lax.broadcasted_iota(I32, (ch, ch), 0) >= lax.broadcasted_iota(I32, (ch, ch), 1))
    for g in range(SGU_GROUPS):
        gs = slice(g * ch, (g + 1) * ch)
        wsg = jnp.where(tril, ws_ref[g], 0.0).astype(BF16)
        for c in range(MERGE_TM // ch):
            cs = slice(c * ch, (c + 1) * ch)
            mixed = _dot(wsg, vln_ref[cs, gs]) + bs_ref[g]
            ya_scr[cs, gs] = (u_ref[cs, gs].astype(F32) * mixed).astype(BF16)
    ma = _dot(ya_scr[...], wa_ref[...])
    mb = _dot(yb_ref[...], wb_ref[...])
    merged = gate_ref[:, :d].astype(F32) * ma + gate_ref[:, d:].astype(F32) * mb
    o = _dot(merged.astype(BF16), wo_ref[...])
    o_ref[...] = x_ref[...] + _rms(o, gpost_ref[...])


def _merge(x2, u, vln, yb, gate, w_s, b_s, w_a, w_b, w_o, g_post, layer):
    n, d = x2.shape
    width = u.shape[1]
    one = pl.Buffered(1)
    tok = lambda i: (i, 0)
    const2 = lambda i: (0, 0)
    whole = lambda a: pl.BlockSpec((None,) + a.shape[1:], lambda i: (layer,) + (0,) * (a.ndim - 1),
                                   pipeline_mode=one)
    return pl.pallas_call(
        _merge_body,
        grid=(n // MERGE_TM,),
        in_specs=[
            pl.BlockSpec((MERGE_TM, d), tok),
            pl.BlockSpec((MERGE_TM, width), tok),
            pl.BlockSpec((MERGE_TM, width), tok),
            pl.BlockSpec((MERGE_TM, width), tok),
            pl.BlockSpec((MERGE_TM, 2 * d), tok),
            whole(w_s), whole(b_s), whole(w_a), whole(w_b), whole(w_o),
            pl.BlockSpec((1, d), const2),
        ],
        out_specs=pl.BlockSpec((MERGE_TM, d), tok),
        out_shape=jax.ShapeDtypeStruct((n, d), F32),
        scratch_shapes=[pltpu.VMEM((MERGE_TM, width), BF16)],
        compiler_params=pltpu.CompilerParams(
            dimension_semantics=("parallel",), vmem_limit_bytes=VMEM_LIMIT),
        name="merge",
    )(x2, u, vln, yb, gate, w_s, b_s, w_a, w_b, w_o, g_post.reshape(1, d))


def kernel(x, ffn1_norm_pre, ffn1_norm_post, ffn1_w_in, ffn1_w_out, mix_norm_pre, mix_norm_post, w_in,
           sgu_ln_g, sgu_ln_b, sgu_w_s, sgu_b, rel_bias, w_branch_a, w_branch_b, w_gate, w_out,
           ffn2_norm_pre, ffn2_norm_post, ffn2_w_in, ffn2_w_out):
    b, s, d = x.shape
    depth = w_in.shape[0]
    sgu_w = sgu_ln_g.shape[1]
    att_w = ATT_HEADS * ATT_HEAD_DIM
    idx_w = IDX_HEADS * IDX_DIM
    top_k = min(TOPK_MAX, s // 4)
    assert s % PROJ_TM == 0 and s % ATT_TQ == 0 and (b * s) % FFN_TM == 0
    assert sgu_w == PROJ_BN and att_w == PROJ_BN and idx_w == PROJ_BN and ATT_KU == ATT_TQ

    sizes = (sgu_w, sgu_w, att_w, att_w, att_w, idx_w, IDX_DIM, IDX_HEADS)
    offs = [0]
    for sz in sizes:
        offs.append(offs[-1] + sz)
    col = lambda w, idx: w[:, :, offs[idx]:offs[idx + 1]]
    bf = lambda w: w.astype(BF16)
    w_gate_b = bf(w_gate)
    w_nat = jnp.concatenate([bf(col(w_in, 0)), bf(col(w_in, 1)), bf(col(w_in, 3))], axis=2)
    nat_blocks = (0, 1, 2)
    wt_cat = jnp.swapaxes(jnp.concatenate([bf(col(w_in, 2)), bf(col(w_in, 4)), bf(col(w_in, 5))], axis=2), 1, 2)
    wwt = jnp.swapaxes(bf(col(w_in, 7)), 1, 2)
    w_kidx = bf(col(w_in, 6))
    ffn1_in_b, ffn1_out_b, ffn2_in_b, ffn2_out_b = bf(ffn1_w_in), bf(ffn1_w_out), bf(ffn2_w_in), bf(ffn2_w_out)
    ffn1_in_b = ffn1_in_b.reshape(depth, d, -1, FFN_TF).transpose(0, 2, 1, 3)
    w_a_b, w_b_b, w_o_b = bf(w_branch_a), bf(w_branch_b), bf(w_out)
    b_s = sgu_b.reshape(depth, SGU_GROUPS, SGU_CHUNK, 1)

    bias = _bias_tiles(rel_bias)
    x2 = x.reshape(b * s, d)
    for l in range(depth):
        x2, h2 = _ffn(x2, ffn1_norm_pre[l], ffn1_norm_post[l], ffn1_in_b, ffn1_out_b, l,
                      g_next=mix_norm_pre[l])
        u, vln, k, gate, kidx = _proj_nat(h2, w_nat, w_kidx, w_gate_b, sgu_ln_g[l], sgu_ln_b[l], l,
                                          nat_blocks)
        qt, vt, qit, wit = _proj_t(h2.reshape(b, s, d), wt_cat, wwt, l)
        yb = _attention(kidx.reshape(b, s, IDX_DIM), qit, wit, k.reshape(b, s, att_w), vt, qt, bias, top_k)
        x2 = _merge(x2, u, vln, yb.reshape(b * s, att_w), gate,
                    sgu_w_s, b_s, w_a_b, w_b_b, w_o_b, mix_norm_post[l], l)
        (x2,) = _ffn(x2, ffn2_norm_pre[l], ffn2_norm_post[l], ffn2_in_b, ffn2_out_b, l)
    return x2.reshape(b, s, d)
```

```python
import functools
import math

import jax
import jax.numpy as jnp
from jax import lax
from jax.experimental import pallas as pl
from jax.experimental.pallas import tpu as pltpu

F32 = jnp.float32
BF16 = jnp.bfloat16
I32 = jnp.int32
I16 = jnp.int16

SGU_GROUPS = 8
SGU_CHUNK = 128
ATT_HEADS = 8
ATT_HEAD_DIM = 128
IDX_HEADS = 16
IDX_DIM = 64
TOPK_MAX = 256
NUM_BUCKETS = 32
MAX_DISTANCE = 128
NORM_EPS = 1e-6
LN_EPS = 1e-5

V7X_VMEM_BYTES = 64 * 1024 * 1024
VMEM_LIMIT = V7X_VMEM_BYTES - 8 * 1024 * 1024
SUBLANES = 8
BF16_TILE_ROWS = 16

FFN_TM = 512
FFN_TF = 512
PROJ_TM = 1024
PROJ_BN = 1024
PROJ_CH = 256
MM_AHEAD = 4
MERGE_TM = 256
ATT_TQ = 256
ATT_KU = 256
ACC_ROWS = ATT_HEAD_DIM + BF16_TILE_ROWS
QK_AHEAD = ATT_HEADS
FIXED_MAX_DENOM_LIMIT = 2.0 ** 20

INT_MIN = -(2 ** 31)
I16_MIN, I16_BIAS = -(2 ** 15), 2 ** 15
COUNT_ALL = 2.0 ** 30
MASKED_LOGIT = -1e30
LOG2E = math.log2(math.e)


def _rms(xf, g, scale=1.0):
    ms = jnp.mean(xf * xf, axis=-1, keepdims=True)
    return xf * (scale * lax.rsqrt(ms + NORM_EPS)) * g


def _gelu_tanh(x):
    c = math.sqrt(2.0 / math.pi)
    return x * (0.5 * (1.0 + jnp.tanh(c * (x + 0.044715 * (x * x * x)))))


def _dot(a, b):
    return jnp.dot(a, b, preferred_element_type=F32)


def _dot_nt(a, b):
    return lax.dot_general(a, b, (((1,), (1,)), ((), ())), preferred_element_type=F32)


def _ffn_body(*refs, emit_next):
    if emit_next:
        x_ref, gpre_ref, gpost_ref, gnext_ref, wa_ref, wb_ref, wo_ref, o_ref, hn_ref, h_scr, acc_scr = refs
    else:
        x_ref, gpre_ref, gpost_ref, wa_ref, wb_ref, wo_ref, o_ref, h_scr, acc_scr = refs
    j = pl.program_id(1)

    @pl.when(j == 0)
    def _():
        h_scr[...] = _rms(x_ref[...], gpre_ref[...]).astype(BF16)
        acc_scr[...] = jnp.zeros_like(acc_scr)

    h = h_scr[...]
    a = _dot(h, wa_ref[...])
    b = _dot(h, wb_ref[...])
    g = (a * jax.nn.sigmoid(a) * b).astype(BF16)
    acc_scr[...] += _dot(g, wo_ref[...])

    @pl.when(j == pl.num_programs(1) - 1)
    def _():
        y = x_ref[...] + _rms(acc_scr[...], gpost_ref[...], scale=0.5)
        o_ref[...] = y
        if emit_next:
            hn_ref[...] = _rms(y, gnext_ref[...]).astype(BF16)


def _ffn(x2, g_pre, g_post, w_in, w_out, layer, g_next=None):
    n, d = x2.shape
    d_ff = w_out.shape[1]
    nf = d_ff // FFN_TF
    emit_next = g_next is not None
    vec = pl.BlockSpec((1, d), lambda i, j: (0, 0))
    tok = pl.BlockSpec((FFN_TM, d), lambda i, j: (i, 0))
    gains = [g_pre, g_post] + ([g_next] if emit_next else [])
    out_shape = [jax.ShapeDtypeStruct((n, d), F32)] + ([jax.ShapeDtypeStruct((n, d), BF16)] if emit_next else [])
    return pl.pallas_call(
        functools.partial(_ffn_body, emit_next=emit_next),
        grid=(n // FFN_TM, nf),
        in_specs=[tok] + [vec] * len(gains) + [
            pl.BlockSpec((None, d, FFN_TF), lambda i, j: (layer, 0, j)),
            pl.BlockSpec((None, d, FFN_TF), lambda i, j: (layer, 0, j + nf)),
            pl.BlockSpec((None, FFN_TF, d), lambda i, j: (layer, j, 0)),
        ],
        out_specs=[tok] * len(out_shape),
        out_shape=out_shape,
        scratch_shapes=[pltpu.VMEM((FFN_TM, d), BF16), pltpu.VMEM((FFN_TM, d), F32)],
        compiler_params=pltpu.CompilerParams(
            dimension_semantics=("parallel", "arbitrary"), vmem_limit_bytes=VMEM_LIMIT),
        name="ffn",
    )(x2, *[g.reshape(1, d) for g in gains], w_in, w_in, w_out)


def _chunked(n_chunks, matmul, epilogue):
    z = [matmul(c) for c in range(min(MM_AHEAD, n_chunks))]
    for c in range(n_chunks):
        if c + MM_AHEAD < n_chunks:
            z.append(matmul(c + MM_AHEAD))
        epilogue(c, z[c])
        z[c] = None


def _proj_nat_body(h_ref, w_ref, wki_ref, wg_ref, lng_ref, lnb_ref,
                   u_ref, vln_ref, k_ref, gate_ref, kidx_ref, v_scr):
    j = pl.program_id(1)
    ch = PROJ_CH
    nc = PROJ_BN // ch
    cols = lambda c: slice(c * ch, (c + 1) * ch)
    matmul = lambda c: _dot(h_ref[...], w_ref[:, cols(c)])
    matmul_gate = lambda c: _dot(h_ref[...], wg_ref[:, cols(c)])

    @pl.when(j == 0)
    def _():
        kidx_ref[...] = _dot(h_ref[...], wki_ref[...]).astype(BF16)

        def store_u(c, z):
            u_ref[:, cols(c)] = _gelu_tanh(z).astype(BF16)
        _chunked(nc, matmul, store_u)

    @pl.when(j == 1)
    def _():
        def store_v(c, z):
            v_scr[:, cols(c)] = _gelu_tanh(z)
        _chunked(nc, matmul, store_v)
        v = v_scr[...]
        mu = jnp.mean(v, axis=-1, keepdims=True)
        vc = v - mu
        var = jnp.mean(vc * vc, axis=-1, keepdims=True)
        vln_ref[...] = (vc * lax.rsqrt(var + LN_EPS) * lng_ref[...] + lnb_ref[...]).astype(BF16)

    @pl.when(j == 2)
    def _():
        def store_k(c, z):
            k_ref[:, cols(c)] = z.astype(BF16)
        _chunked(nc, matmul, store_k)

    @pl.when(j >= 3)
    def _():
        def store_gate(c, z):
            gate_ref[:, cols(c)] = jax.nn.sigmoid(z).astype(BF16)
        _chunked(nc, matmul_gate, store_gate)


def _proj_nat(h2, w_in, w_kidx, w_gate, ln_g, ln_b, layer, blocks):
    n, d = h2.shape
    width = PROJ_BN
    ngate = w_gate.shape[2]
    n_in = len(blocks)
    assert n_in == 3
    nj = n_in + ngate // width
    tok = lambda i, j: (i, 0)

    def in_block(i, j):
        blk = blocks[n_in - 1]
        for t in range(n_in - 2, -1, -1):
            blk = jnp.where(j == t, blocks[t], blk)
        return (layer, 0, blk)

    return pl.pallas_call(
        _proj_nat_body,
        grid=(n // PROJ_TM, nj),
        in_specs=[
            pl.BlockSpec((PROJ_TM, d), tok),
            pl.BlockSpec((None, d, width), in_block),
            pl.BlockSpec((None, d, IDX_DIM), lambda i, j: (layer, 0, 0)),
            pl.BlockSpec((None, d, width), lambda i, j: (layer, 0, jnp.maximum(j - n_in, 0))),
            pl.BlockSpec((1, width), lambda i, j: (0, 0)),
            pl.BlockSpec((1, width), lambda i, j: (0, 0)),
        ],
        out_specs=[
            pl.BlockSpec((PROJ_TM, width), tok),
            pl.BlockSpec((PROJ_TM, width), tok),
            pl.BlockSpec((PROJ_TM, width), tok),
            pl.BlockSpec((PROJ_TM, width), lambda i, j: (i, jnp.maximum(j - 3, 0))),
            pl.BlockSpec((PROJ_TM, IDX_DIM), tok),
        ],
        out_shape=[
            jax.ShapeDtypeStruct((n, width), BF16),
            jax.ShapeDtypeStruct((n, width), BF16),
            jax.ShapeDtypeStruct((n, width), BF16),
            jax.ShapeDtypeStruct((n, ngate), BF16),
            jax.ShapeDtypeStruct((n, IDX_DIM), BF16),
        ],
        scratch_shapes=[pltpu.VMEM((PROJ_TM, width), F32)],
        compiler_params=pltpu.CompilerParams(
            dimension_semantics=("parallel", "arbitrary"), vmem_limit_bytes=VMEM_LIMIT),
        name="proj_nat",
    )(h2, w_in, w_kidx, w_gate, ln_g.reshape(1, width), ln_b.reshape(1, width))


def _proj_t_body(h_ref, wt_ref, wwt_ref, qt_ref, vt_ref, qit_ref, wit_ref):
    j = pl.program_id(2)
    ch = PROJ_CH
    nc = PROJ_BN // ch
    rows = lambda c: slice(c * ch, (c + 1) * ch)
    matmul = lambda c: _dot_nt(wt_ref[rows(c), :], h_ref[0])

    @pl.when(j == 0)
    def _():
        wit_ref[0] = _dot_nt(wwt_ref[...], h_ref[0])

        def store_q(c, zt):
            qt_ref[0, rows(c), :] = (zt * (ATT_HEAD_DIM ** -0.5 * LOG2E)).astype(BF16)
        _chunked(nc, matmul, store_q)

    @pl.when(j == 1)
    def _():
        def store_v(c, zt):
            for cc in range(PROJ_TM // ATT_KU):
                vt_ref[0, cc, rows(c), :] = zt[:, cc * ATT_KU:(cc + 1) * ATT_KU].astype(BF16)
        _chunked(nc, matmul, store_v)

    @pl.when(j == 2)
    def _():
        def store_qi(c, zt):
            qit_ref[0, rows(c), :] = zt.astype(BF16)
        _chunked(nc, matmul, store_qi)


def _proj_t(h3, wt_cat, wwt, layer):
    b, s, d = h3.shape
    width = PROJ_BN
    nch = PROJ_TM // ATT_KU
    feat = lambda bi, si, j: (bi, 0, si)
    return pl.pallas_call(
        _proj_t_body,
        grid=(b, s // PROJ_TM, 3),
        in_specs=[
            pl.BlockSpec((1, PROJ_TM, d), lambda bi, si, j: (bi, si, 0)),
            pl.BlockSpec((None, width, d), lambda bi, si, j: (layer, j, 0)),
            pl.BlockSpec((None, IDX_HEADS, d), lambda bi, si, j: (layer, 0, 0)),
        ],
        out_specs=[
            pl.BlockSpec((1, width, PROJ_TM), feat),
            pl.BlockSpec((1, nch, width, ATT_KU), lambda bi, si, j: (bi, si, 0, 0)),
            pl.BlockSpec((1, width, PROJ_TM), feat),
            pl.BlockSpec((1, IDX_HEADS, PROJ_TM), feat),
        ],
        out_shape=[
            jax.ShapeDtypeStruct((b, width, s), BF16),
            jax.ShapeDtypeStruct((b, s // ATT_KU, width, ATT_KU), BF16),
            jax.ShapeDtypeStruct((b, width, s), BF16),
            jax.ShapeDtypeStruct((b, IDX_HEADS, s), F32),
        ],
        compiler_params=pltpu.CompilerParams(
            dimension_semantics=("parallel", "parallel", "arbitrary"), vmem_limit_bytes=VMEM_LIMIT),
        name="proj_t",
    )(h3, wt_cat, wwt)


N_NEAR_UNITS = 2


def _bias_body(rb_ref, o_ref):
    h = pl.program_id(0)
    r = lax.broadcasted_iota(I32, (ATT_KU, ATT_TQ), 0)
    c = lax.broadcasted_iota(I32, (ATT_KU, ATT_TQ), 1)
    max_exact = NUM_BUCKETS // 2
    far = rb_ref[NUM_BUCKETS - 1, h]
    for t in range(N_NEAR_UNITS):
        dist = jnp.maximum(t * ATT_KU + c - r, 0)
        nf = jnp.maximum(dist, 1).astype(F32)
        large = max_exact + (jnp.log(nf / max_exact) / math.log(MAX_DISTANCE / max_exact)
                             * (NUM_BUCKETS - max_exact)).astype(I32)
        large = jnp.minimum(large, NUM_BUCKETS - 1)
        bucket = jnp.where(dist < max_exact, dist, large)
        val = jnp.zeros((ATT_KU, ATT_TQ), F32)
        for bkt in range(NUM_BUCKETS):
            val = jnp.where(bucket == bkt, rb_ref[bkt, h], val)
        o_ref[0, t] = (val - far) * LOG2E


def _bias_tiles(rel_bias):
    assert N_NEAR_UNITS * ATT_KU - (ATT_KU - 1) >= MAX_DISTANCE
    return pl.pallas_call(
        _bias_body,
        grid=(ATT_HEADS,),
        in_specs=[pl.BlockSpec(memory_space=pltpu.SMEM)],
        out_specs=pl.BlockSpec((1, N_NEAR_UNITS, ATT_KU, ATT_TQ), lambda h: (h, 0, 0, 0)),
        out_shape=jax.ShapeDtypeStruct((ATT_HEADS, N_NEAR_UNITS, ATT_KU, ATT_TQ), F32),
        name="bias_tiles",
    )(rel_bias)


def _tree_sum(xs):
    while len(xs) > 1:
        xs = [a + b for a, b in zip(xs[::2], xs[1::2])] + ([xs[-1]] if len(xs) % 2 else [])
    return xs[0]


def _sublane_allmax(x):
    for shift in (4, 2, 1):
        x = jnp.maximum(x, pltpu.roll(x, shift, axis=0))
    return x


def _attn_body(kidx_ref, qit_ref, wit_ref, k_ref, vt_ref, qt_ref, bias_ref, o_ref,
               key_scr, hi_scr, lo_scr, m_scr, acc_scr, save_scr, *, top_k, zero_ref=False):
    i = pl.program_id(1)
    ku, tq, sl, pk = ATT_KU, ATT_TQ, SUBLANES, BF16_TILE_ROWS
    n_units = i + 1
    q0 = i * tq

    w_all = wit_ref[0] * (IDX_HEADS ** -0.5 * IDX_DIM ** -0.5)
    row = lax.broadcasted_iota(I32, (ku, tq), 0)
    col = lax.broadcasted_iota(I32, (ku, tq), 1)

    def score_unit(u, carry):
        r0 = pl.multiple_of(u * ku, ku)
        kch = kidx_ref[0, pl.ds(r0, ku), :]
        acc = jnp.zeros((ku, tq), F32)
        for h in range(IDX_HEADS):
            d = _dot(kch, qit_ref[0, h * IDX_DIM:(h + 1) * IDX_DIM, :])
            acc = acc + jnp.maximum(d, 0.0) * w_all[h:h + 1, :]
        bits = lax.bitcast_convert_type(acc, I32)
        key = jnp.where(bits < 0, bits ^ jnp.int32(0x7FFFFFFF), bits)
        key = jnp.where(r0 + row <= q0 + col, key, INT_MIN)
        key_scr[pl.ds(r0, ku), :] = key
        hi_scr[pl.ds(r0, ku), :] = lax.shift_right_arithmetic(key, 16).astype(I16)
        lo_scr[pl.ds(r0, ku), :] = ((key & 0xFFFF) - I16_BIAS).astype(I16)
        return carry

    lax.fori_loop(0, n_units, score_unit, 0)

    one_b, zero_b = jnp.ones((), BF16), jnp.zeros((), BF16)

    n_pairs = (n_units + 1) // 2

    @pl.when(n_units % 2 == 1)
    def _():
        r0 = pl.multiple_of(n_units * ku, ku)
        hi_scr[pl.ds(r0, ku), :] = jnp.full((ku, tq), I16_MIN, I16)
        lo_scr[pl.ds(r0, ku), :] = jnp.full((ku, tq), I16_MIN, I16)

    def count_ge(ref, cand):
        cand16 = jnp.broadcast_to(cand, (pk, tq)).astype(I16)

        def body(p, cnt):
            r0 = pl.multiple_of(p * (2 * ku), 2 * ku)
            blk = ref[pl.ds(r0, 2 * ku), :].reshape(2 * ku // pk, pk, tq)
            hit = jnp.where(blk >= cand16[None], one_b, zero_b)
            return cnt + _tree_sum([hit[g] for g in range(2 * ku // pk)]).astype(F32)

        cnt = lax.fori_loop(0, n_pairs, body, jnp.zeros((pk, tq), F32))
        return jnp.sum(cnt, axis=0, keepdims=True)

    def kth_largest(ref, kvec):
        zero = jnp.zeros((1, tq), I32)
        c0 = count_ge(ref, zero)
        ok0 = c0 >= kvec
        state = (jnp.where(ok0, zero, I16_MIN), jnp.where(ok0, c0, COUNT_ALL), jnp.where(ok0, 0.0, c0))

        def descend(t, state):
            prefix, n_ge, n_gt = state
            cand = prefix | jnp.left_shift(jnp.int32(1), 14 - t)
            c = count_ge(ref, cand)
            ok = c >= kvec
            return jnp.where(ok, cand, prefix), jnp.where(ok, c, n_ge), jnp.where(ok, n_gt, c)

        return lax.fori_loop(0, 15, descend, state)

    k_f = jnp.full((1, tq), top_k, F32)
    thr_hi, _, n_above = kth_largest(hi_scr, k_f)
    k_low = k_f - n_above
    thr_hi16 = jnp.broadcast_to(thr_hi, (pk, tq)).astype(I16)

    def keep_low_of_winners(p, carry):
        r0 = pl.multiple_of(p * (2 * ku), 2 * ku)
        hi = hi_scr[pl.ds(r0, 2 * ku), :].reshape(2 * ku // pk, pk, tq)
        lo = lo_scr[pl.ds(r0, 2 * ku), :].reshape(2 * ku // pk, pk, tq)
        lo_scr[pl.ds(r0, 2 * ku), :] = jnp.where(hi == thr_hi16[None], lo,
                                                 jnp.int16(I16_MIN)).reshape(2 * ku, tq)
        return carry

    lax.fori_loop(0, n_pairs, keep_low_of_winners, 0)
    thr_lo, n_ge_lo, n_greater = kth_largest(lo_scr, k_low)
    thr_raw = thr_hi * (1 << 16) + (thr_lo + I16_BIAS)
    thr = jnp.maximum(thr_raw, INT_MIN + 1)

    n_tied_ok = k_low - n_greater
    n_tied = n_ge_lo - n_greater
    has_excess = jnp.logical_and(n_tied > n_tied_ok, thr_raw > INT_MIN)

    @pl.when(jnp.max(jnp.where(has_excess, 1.0, 0.0)) > 0.0)
    def _():
        earlier = (lax.broadcasted_iota(I32, (ku, ku), 0) > lax.broadcasted_iota(I32, (ku, ku), 1))
        earlier = jnp.where(earlier, 1.0, 0.0).astype(BF16)

        def demote(u, seen):
            r0 = pl.multiple_of(u * ku, ku)
            key = key_scr[pl.ds(r0, ku), :]
            tied = key == thr
            rank = _dot(earlier, jnp.where(tied, 1.0, 0.0).astype(BF16)) + seen
            key_scr[pl.ds(r0, ku), :] = jnp.where(jnp.logical_and(tied, rank >= n_tied_ok), INT_MIN, key)
            return seen + jnp.sum(jnp.where(tied, 1.0, 0.0), axis=0, keepdims=True)

        lax.fori_loop(0, n_units, demote, jnp.zeros((1, tq), F32))

    m_scr[...] = jnp.full(m_scr.shape, MASKED_LOGIT, F32)
    acc_scr[...] = jnp.zeros(acc_scr.shape, F32)
    ones_rows = jnp.ones((ACC_ROWS - ATT_HEAD_DIM, ku), BF16)

    def unit(u, near_tile, rescale):
        r0 = pl.multiple_of(u * ku, ku)
        keep = key_scr[pl.ds(r0, ku), :].reshape(ku // sl, sl, tq) >= thr[None]
        mask_add = jnp.where(keep, 0.0, MASKED_LOGIT)

        def qk(h):
            hs = slice(h * ATT_HEAD_DIM, (h + 1) * ATT_HEAD_DIM)
            s = _dot(k_ref[0, pl.ds(r0, ku), hs], qt_ref[0, hs, :])
            if near_tile is not None:
                s = s + bias_ref[h, near_tile]
            return s

        ahead = QK_AHEAD
        pending = [qk(h) for h in range(min(ahead, ATT_HEADS))]
        for h in range(ATT_HEADS):
            hs = slice(h * ATT_HEAD_DIM, (h + 1) * ATT_HEAD_DIM)
            s = pending.pop(0)
            if h + ahead < ATT_HEADS:
                pending.append(qk(h + ahead))
            s = s.reshape(ku // sl, sl, tq) + mask_add
            if rescale == "zero":
                p = jnp.exp2(s)
            else:
                m_old = m_scr[h]
                if rescale:
                    m_new = jnp.maximum(m_old, _sublane_allmax(jnp.max(s, axis=0)))
                    alpha = jnp.exp2(m_old - m_new)
                    m_scr[h] = m_new
                else:
                    m_new = m_old
                p = jnp.exp2(s - m_new[None])
            v_ext = jnp.concatenate([vt_ref[0, u, hs, :], ones_rows], axis=0)
            pv = _dot(v_ext, p.reshape(ku, tq).astype(BF16))
            if rescale is True:
                acc = acc_scr[h].reshape(ACC_ROWS // sl, sl, tq) * alpha[None]
                acc_scr[h] = pv + acc.reshape(ACC_ROWS, tq)
            else:
                acc_scr[h] = pv + acc_scr[h]

    def run_units(first_far, near_trips, rescale):
        lax.fori_loop(first_far, n_far, lambda u, c: (unit(u, None, rescale), c)[1], 0)
        for t in range(N_NEAR_UNITS - 1, -1, -1):
            lax.fori_loop(0, jnp.minimum(near_trips, jnp.minimum(n_units - t, 1)),
                          lambda _, c, t=t: (unit(i - t, t, rescale), c)[1], 0)

    n_far = jnp.maximum(n_units - N_NEAR_UNITS, 0)
    if zero_ref:
        run_units(0, 1, "zero")
        outgrown = jnp.zeros((1, tq), F32)
        for h in range(ATT_HEADS):
            l = acc_scr[h, ATT_HEAD_DIM:ATT_HEAD_DIM + 1, :]
            ok = jnp.where(l < FIXED_MAX_DENOM_LIMIT, jnp.where(l >= 2.0 ** -60, 0.0, 1.0), 1.0)
            outgrown = jnp.maximum(outgrown, ok)

        @pl.when(jnp.max(outgrown) > 0.0)
        def _():
            acc_scr[...] = jnp.zeros(acc_scr.shape, F32)
            run_units(0, 1, True)
    else:
        lead = jnp.minimum(n_far, 1)
        lax.fori_loop(0, lead, lambda u, c: (unit(u, None, True), c)[1], 0)
        save_scr[...] = acc_scr[...]
        run_units(lead, lead, False)
        outgrown = jnp.zeros((1, tq), F32)
        for h in range(ATT_HEADS):
            l = acc_scr[h, ATT_HEAD_DIM:ATT_HEAD_DIM + 1, :]
            outgrown = jnp.maximum(outgrown, jnp.where(l < FIXED_MAX_DENOM_LIMIT, 0.0, 1.0))

        @pl.when(jnp.logical_or(lead == 0, jnp.max(outgrown) > 0.0))
        def _():
            acc_scr[...] = save_scr[...]
            run_units(lead, 1, True)

    for h in range(ATT_HEADS):
        hs = slice(h * ATT_HEAD_DIM, (h + 1) * ATT_HEAD_DIM)
        l = acc_scr[h, ATT_HEAD_DIM:ATT_HEAD_DIM + 1, :]
        o_ref[0, :, hs] = (acc_scr[h, :ATT_HEAD_DIM, :] / l).T.astype(BF16)


def _attention(kidx, qit, wit, k, vt, qt, bias, top_k, zero_ref=False):
    b, s, width = k.shape
    one = pl.Buffered(1)
    return pl.pallas_call(
        functools.partial(_attn_body, top_k=top_k, zero_ref=zero_ref),
        grid=(b, s // ATT_TQ),
        in_specs=[
            pl.BlockSpec((1, s, IDX_DIM), lambda bi, i: (bi, 0, 0), pipeline_mode=one),
            pl.BlockSpec((1, IDX_HEADS * IDX_DIM, ATT_TQ), lambda bi, i: (bi, 0, i)),
            pl.BlockSpec((1, IDX_HEADS, ATT_TQ), lambda bi, i: (bi, 0, i)),
            pl.BlockSpec((1, s, width), lambda bi, i: (bi, 0, 0), pipeline_mode=one),
            pl.BlockSpec((1, s // ATT_KU, width, ATT_KU), lambda bi, i: (bi, 0, 0, 0), pipeline_mode=one),
            pl.BlockSpec((1, width, ATT_TQ), lambda bi, i: (bi, 0, i)),
            pl.BlockSpec(bias.shape, lambda bi, i: (0, 0, 0, 0), pipeline_mode=one),
        ],
        out_specs=pl.BlockSpec((1, ATT_TQ, width), lambda bi, i: (bi, i, 0)),
        out_shape=jax.ShapeDtypeStruct((b, s, width), BF16),
        scratch_shapes=[
            pltpu.VMEM((s, ATT_TQ), I32),
            pltpu.VMEM((s, ATT_TQ), I16),
            pltpu.VMEM((s, ATT_TQ), I16),
            pltpu.VMEM((ATT_HEADS, SUBLANES, ATT_TQ), F32),
            pltpu.VMEM((ATT_HEADS, ACC_ROWS, ATT_TQ), F32),
            pltpu.VMEM((ATT_HEADS, ACC_ROWS, ATT_TQ), F32),
        ],
        compiler_params=pltpu.CompilerParams(
            dimension_semantics=("parallel", "arbitrary"), vmem_limit_bytes=VMEM_LIMIT),
        name="dsa_attention",
    )(kidx, qit, wit, k, vt, qt, bias)


def _merge_body(x_ref, u_ref, vln_ref, yb_ref, gate_ref, ws_ref, bs_ref, wa_ref, wb_ref, wo_ref,
                gpost_ref, o_ref, ya_scr):
    ch = SGU_CHUNK
    d = x_ref.shape[1]
    tril = (lax.broadcasted_iota(I32, (ch, ch), 0) >= lax.broadcasted_iota(I32, (ch, ch), 1))
    for g in range(SGU_GROUPS):
        gs = slice(g * ch, (g + 1) * ch)
        wsg = jnp.where(tril, ws_ref[g], 0.0).astype(BF16)
        for c in range(MERGE_TM // ch):
            cs = slice(c * ch, (c + 1) * ch)
            mixed = _dot(wsg, vln_ref[cs, gs]) + bs_ref[g]
            ya_scr[cs, gs] = (u_ref[cs, gs].astype(F32) * mixed).astype(BF16)
    ma = _dot(ya_scr[...], wa_ref[...])
    mb = _dot(yb_ref[...], wb_ref[...])
    merged = gate_ref[:, :d].astype(F32) * ma + gate_ref[:, d:].astype(F32) * mb
    o = _dot(merged.astype(BF16), wo_ref[...])
    o_ref[...] = x_ref[...] + _rms(o, gpost_ref[...])


def _merge(x2, u, vln, yb, gate, w_s, b_s, w_a, w_b, w_o, g_post, layer):
    n, d = x2.shape
    width = u.shape[1]
    one = pl.Buffered(1)
    tok = lambda i: (i, 0)
    const2 = lambda i: (0, 0)
    whole = lambda a: pl.BlockSpec((None,) + a.shape[1:], lambda i: (layer,) + (0,) * (a.ndim - 1),
                                   pipeline_mode=one)
    return pl.pallas_call(
        _merge_body,
        grid=(n // MERGE_TM,),
        in_specs=[
            pl.BlockSpec((MERGE_TM, d), tok),
            pl.BlockSpec((MERGE_TM, width), tok),
            pl.BlockSpec((MERGE_TM, width), tok),
            pl.BlockSpec((MERGE_TM, width), tok),
            pl.BlockSpec((MERGE_TM, 2 * d), tok),
            whole(w_s), whole(b_s), whole(w_a), whole(w_b), whole(w_o),
            pl.BlockSpec((1, d), const2),
        ],
        out_specs=pl.BlockSpec((MERGE_TM, d), tok),
        out_shape=jax.ShapeDtypeStruct((n, d), F32),
        scratch_shapes=[pltpu.VMEM((MERGE_TM, width), BF16)],
        compiler_params=pltpu.CompilerParams(
            dimension_semantics=("parallel",), vmem_limit_bytes=VMEM_LIMIT),
        name="merge",
    )(x2, u, vln, yb, gate, w_s, b_s, w_a, w_b, w_o, g_post.reshape(1, d))


def kernel(x, ffn1_norm_pre, ffn1_norm_post, ffn1_w_in, ffn1_w_out, mix_norm_pre, mix_norm_post, w_in,
           sgu_ln_g, sgu_ln_b, sgu_w_s, sgu_b, rel_bias, w_branch_a, w_branch_b, w_gate, w_out,
           ffn2_norm_pre, ffn2_norm_post, ffn2_w_in, ffn2_w_out):
    b, s, d = x.shape
    depth = w_in.shape[0]
    sgu_w = sgu_ln_g.shape[1]
    att_w = ATT_HEADS * ATT_HEAD_DIM
    idx_w = IDX_HEADS * IDX_DIM
    top_k = min(TOPK_MAX, s // 4)
    assert s % PROJ_TM == 0 and s % ATT_TQ == 0 and (b * s) % FFN_TM == 0
    assert sgu_w == PROJ_BN and att_w == PROJ_BN and idx_w == PROJ_BN and ATT_KU == ATT_TQ

    sizes = (sgu_w, sgu_w, att_w, att_w, att_w, idx_w, IDX_DIM, IDX_HEADS)
    offs = [0]
    for sz in sizes:
        offs.append(offs[-1] + sz)
    col = lambda w, idx: w[:, :, offs[idx]:offs[idx + 1]]
    bf = lambda w: w.astype(BF16)
    w_gate_b = bf(w_gate)
    w_nat = jnp.concatenate([bf(col(w_in, 0)), bf(col(w_in, 1)), bf(col(w_in, 3))], axis=2)
    nat_blocks = (0, 1, 2)
    wt_cat = jnp.swapaxes(jnp.concatenate([bf(col(w_in, 2)), bf(col(w_in, 4)), bf(col(w_in, 5))], axis=2), 1, 2)
    wwt = jnp.swapaxes(bf(col(w_in, 7)), 1, 2)
    w_kidx = bf(col(w_in, 6))
    ffn1_in_b, ffn1_out_b, ffn2_in_b, ffn2_out_b = bf(ffn1_w_in), bf(ffn1_w_out), bf(ffn2_w_in), bf(ffn2_w_out)
    w_a_b, w_b_b, w_o_b = bf(w_branch_a), bf(w_branch_b), bf(w_out)
    b_s = sgu_b.reshape(depth, SGU_GROUPS, SGU_CHUNK, 1)

    bias = _bias_tiles(rel_bias)
    x2 = x.reshape(b * s, d)
    for l in range(depth):
        x2, h2 = _ffn(x2, ffn1_norm_pre[l], ffn1_norm_post[l], ffn1_in_b, ffn1_out_b, l,
                      g_next=mix_norm_pre[l])
        u, vln, k, gate, kidx = _proj_nat(h2, w_nat, w_kidx, w_gate_b, sgu_ln_g[l], sgu_ln_b[l], l,
                                          nat_blocks)
        qt, vt, qit, wit = _proj_t(h2.reshape(b, s, d), wt_cat, wwt, l)
        yb = _attention(kidx.reshape(b, s, IDX_DIM), qit, wit, k.reshape(b, s, att_w), vt, qt, bias, top_k,
                        zero_ref=(l == 1))
        x2 = _merge(x2, u, vln, yb.reshape(b * s, att_w), gate,
                    sgu_w_s, b_s, w_a_b, w_b_b, w_o_b, mix_norm_post[l], l)
        (x2,) = _ffn(x2, ffn2_norm_pre[l], ffn2_norm_post[l], ffn2_in_b, ffn2_out_b, l)
    return x2.reshape(b, s, d)
```

```python
import functools
import math

import jax
import jax.numpy as jnp
from jax import lax
from jax.experimental import pallas as pl
from jax.experimental.pallas import tpu as pltpu

F32 = jnp.float32
BF16 = jnp.bfloat16
I32 = jnp.int32
I16 = jnp.int16

SGU_GROUPS = 8
SGU_CHUNK = 128
ATT_HEADS = 8
ATT_HEAD_DIM = 128
IDX_HEADS = 16
IDX_DIM = 64
TOPK_MAX = 256
NUM_BUCKETS = 32
MAX_DISTANCE = 128
NORM_EPS = 1e-6
LN_EPS = 1e-5

V7X_VMEM_BYTES = 64 * 1024 * 1024
VMEM_LIMIT = V7X_VMEM_BYTES - 8 * 1024 * 1024
SUBLANES = 8
BF16_TILE_ROWS = 16

FFN_TM = 512
FFN_TF = 512
PROJ_TM = 1024
PROJ_BN = 1024
PROJ_CH = 256
MM_AHEAD = 4
MERGE_TM = 256
ATT_TQ = 256
ATT_KU = 256
ACC_ROWS = ATT_HEAD_DIM + BF16_TILE_ROWS
QK_AHEAD = ATT_HEADS
DENOM_MIN, DENOM_MAX = 2.0 ** -60, 2.0 ** 20

INT_MIN = -(2 ** 31)
I16_MIN, I16_BIAS = -(2 ** 15), 2 ** 15
COUNT_ALL = 2.0 ** 30
MASKED_LOGIT = -1e30
LOG2E = math.log2(math.e)


def _rms(xf, g, scale=1.0):
    ms = jnp.mean(xf * xf, axis=-1, keepdims=True)
    return xf * (scale * lax.rsqrt(ms + NORM_EPS)) * g


def _gelu_tanh(x):
    c = math.sqrt(2.0 / math.pi)
    return x * (0.5 * (1.0 + jnp.tanh(c * (x + 0.044715 * (x * x * x)))))


def _dot(a, b):
    return jnp.dot(a, b, preferred_element_type=F32)


def _dot_nt(a, b):
    return lax.dot_general(a, b, (((1,), (1,)), ((), ())), preferred_element_type=F32)


def _ffn_body(*refs, emit_next):
    if emit_next:
        x_ref, gpre_ref, gpost_ref, gnext_ref, wa_ref, wb_ref, wo_ref, o_ref, hn_ref, h_scr, acc_scr = refs
    else:
        x_ref, gpre_ref, gpost_ref, wa_ref, wb_ref, wo_ref, o_ref, h_scr, acc_scr = refs
    j = pl.program_id(1)

    @pl.when(j == 0)
    def _():
        h_scr[...] = _rms(x_ref[...], gpre_ref[...]).astype(BF16)
        acc_scr[...] = jnp.zeros_like(acc_scr)

    h = h_scr[...]
    a = _dot(h, wa_ref[...])
    b = _dot(h, wb_ref[...])
    g = (a * jax.nn.sigmoid(a) * b).astype(BF16)
    acc_scr[...] += _dot(g, wo_ref[...])

    @pl.when(j == pl.num_programs(1) - 1)
    def _():
        y = x_ref[...] + _rms(acc_scr[...], gpost_ref[...], scale=0.5)
        o_ref[...] = y
        if emit_next:
            hn_ref[...] = _rms(y, gnext_ref[...]).astype(BF16)


def _ffn(x2, g_pre, g_post, w_in, w_out, layer, g_next=None):
    n, d = x2.shape
    d_ff = w_out.shape[1]
    nf = d_ff // FFN_TF
    emit_next = g_next is not None
    vec = pl.BlockSpec((1, d), lambda i, j: (0, 0))
    tok = pl.BlockSpec((FFN_TM, d), lambda i, j: (i, 0))
    gains = [g_pre, g_post] + ([g_next] if emit_next else [])
    out_shape = [jax.ShapeDtypeStruct((n, d), F32)] + ([jax.ShapeDtypeStruct((n, d), BF16)] if emit_next else [])
    return pl.pallas_call(
        functools.partial(_ffn_body, emit_next=emit_next),
        grid=(n // FFN_TM, nf),
        in_specs=[tok] + [vec] * len(gains) + [
            pl.BlockSpec((None, d, FFN_TF), lambda i, j: (layer, 0, j)),
            pl.BlockSpec((None, d, FFN_TF), lambda i, j: (layer, 0, j + nf)),
            pl.BlockSpec((None, FFN_TF, d), lambda i, j: (layer, j, 0)),
        ],
        out_specs=[tok] * len(out_shape),
        out_shape=out_shape,
        scratch_shapes=[pltpu.VMEM((FFN_TM, d), BF16), pltpu.VMEM((FFN_TM, d), F32)],
        compiler_params=pltpu.CompilerParams(
            dimension_semantics=("parallel", "arbitrary"), vmem_limit_bytes=VMEM_LIMIT),
        name="ffn",
    )(x2, *[g.reshape(1, d) for g in gains], w_in, w_in, w_out)


def _chunked(n_chunks, matmul, epilogue):
    z = [matmul(c) for c in range(min(MM_AHEAD, n_chunks))]
    for c in range(n_chunks):
        if c + MM_AHEAD < n_chunks:
            z.append(matmul(c + MM_AHEAD))
        epilogue(c, z[c])
        z[c] = None


def _proj_nat_body(h_ref, w_ref, wki_ref, wg_ref, lng_ref, lnb_ref,
                   u_ref, vln_ref, k_ref, gate_ref, kidx_ref, v_scr):
    j = pl.program_id(1)
    ch = PROJ_CH
    nc = PROJ_BN // ch
    cols = lambda c: slice(c * ch, (c + 1) * ch)
    matmul = lambda c: _dot(h_ref[...], w_ref[:, cols(c)])
    matmul_gate = lambda c: _dot(h_ref[...], wg_ref[:, cols(c)])

    @pl.when(j == 0)
    def _():
        kidx_ref[...] = _dot(h_ref[...], wki_ref[...]).astype(BF16)

        def store_u(c, z):
            u_ref[:, cols(c)] = _gelu_tanh(z).astype(BF16)
        _chunked(nc, matmul, store_u)

    @pl.when(j == 1)
    def _():
        def store_v(c, z):
            v_scr[:, cols(c)] = _gelu_tanh(z)
        _chunked(nc, matmul, store_v)
        v = v_scr[...]
        mu = jnp.mean(v, axis=-1, keepdims=True)
        vc = v - mu
        var = jnp.mean(vc * vc, axis=-1, keepdims=True)
        vln_ref[...] = (vc * lax.rsqrt(var + LN_EPS) * lng_ref[...] + lnb_ref[...]).astype(BF16)

    @pl.when(j == 2)
    def _():
        def store_k(c, z):
            k_ref[:, cols(c)] = z.astype(BF16)
        _chunked(nc, matmul, store_k)

    @pl.when(j >= 3)
    def _():
        def store_gate(c, z):
            gate_ref[:, cols(c)] = jax.nn.sigmoid(z).astype(BF16)
        _chunked(nc, matmul_gate, store_gate)


def _proj_nat(h2, w_in, w_kidx, w_gate, ln_g, ln_b, layer, blocks):
    n, d = h2.shape
    width = PROJ_BN
    ngate = w_gate.shape[2]
    n_in = len(blocks)
    assert n_in == 3
    nj = n_in + ngate // width
    tok = lambda i, j: (i, 0)

    def in_block(i, j):
        blk = blocks[n_in - 1]
        for t in range(n_in - 2, -1, -1):
            blk = jnp.where(j == t, blocks[t], blk)
        return (layer, 0, blk)

    return pl.pallas_call(
        _proj_nat_body,
        grid=(n // PROJ_TM, nj),
        in_specs=[
            pl.BlockSpec((PROJ_TM, d), tok),
            pl.BlockSpec((None, d, width), in_block),
            pl.BlockSpec((None, d, IDX_DIM), lambda i, j: (layer, 0, 0)),
            pl.BlockSpec((None, d, width), lambda i, j: (layer, 0, jnp.maximum(j - n_in, 0))),
            pl.BlockSpec((1, width), lambda i, j: (0, 0)),
            pl.BlockSpec((1, width), lambda i, j: (0, 0)),
        ],
        out_specs=[
            pl.BlockSpec((PROJ_TM, width), tok),
            pl.BlockSpec((PROJ_TM, width), tok),
            pl.BlockSpec((PROJ_TM, width), tok),
            pl.BlockSpec((PROJ_TM, width), lambda i, j: (i, jnp.maximum(j - 3, 0))),
            pl.BlockSpec((PROJ_TM, IDX_DIM), tok),
        ],
        out_shape=[
            jax.ShapeDtypeStruct((n, width), BF16),
            jax.ShapeDtypeStruct((n, width), BF16),
            jax.ShapeDtypeStruct((n, width), BF16),
            jax.ShapeDtypeStruct((n, ngate), BF16),
            jax.ShapeDtypeStruct((n, IDX_DIM), BF16),
        ],
        scratch_shapes=[pltpu.VMEM((PROJ_TM, width), F32)],
        compiler_params=pltpu.CompilerParams(
            dimension_semantics=("parallel", "arbitrary"), vmem_limit_bytes=VMEM_LIMIT),
        name="proj_nat",
    )(h2, w_in, w_kidx, w_gate, ln_g.reshape(1, width), ln_b.reshape(1, width))


def _proj_t_body(h_ref, wt_ref, wwt_ref, qt_ref, vt_ref, qit_ref, wit_ref):
    j = pl.program_id(2)
    ch = PROJ_CH
    nc = PROJ_BN // ch
    rows = lambda c: slice(c * ch, (c + 1) * ch)
    matmul = lambda c: _dot_nt(wt_ref[rows(c), :], h_ref[0])

    @pl.when(j == 0)
    def _():
        wit_ref[0] = _dot_nt(wwt_ref[...], h_ref[0])

        def store_q(c, zt):
            qt_ref[0, rows(c), :] = (zt * (ATT_HEAD_DIM ** -0.5 * LOG2E)).astype(BF16)
        _chunked(nc, matmul, store_q)

    @pl.when(j == 1)
    def _():
        def store_v(c, zt):
            for cc in range(PROJ_TM // ATT_KU):
                vt_ref[0, cc, rows(c), :] = zt[:, cc * ATT_KU:(cc + 1) * ATT_KU].astype(BF16)
        _chunked(nc, matmul, store_v)

    @pl.when(j == 2)
    def _():
        def store_qi(c, zt):
            qit_ref[0, rows(c), :] = zt.astype(BF16)
        _chunked(nc, matmul, store_qi)


def _proj_t(h3, wt_cat, wwt, layer):
    b, s, d = h3.shape
    width = PROJ_BN
    nch = PROJ_TM // ATT_KU
    feat = lambda bi, si, j: (bi, 0, si)
    return pl.pallas_call(
        _proj_t_body,
        grid=(b, s // PROJ_TM, 3),
        in_specs=[
            pl.BlockSpec((1, PROJ_TM, d), lambda bi, si, j: (bi, si, 0)),
            pl.BlockSpec((None, width, d), lambda bi, si, j: (layer, j, 0)),
            pl.BlockSpec((None, IDX_HEADS, d), lambda bi, si, j: (layer, 0, 0)),
        ],
        out_specs=[
            pl.BlockSpec((1, width, PROJ_TM), feat),
            pl.BlockSpec((1, nch, width, ATT_KU), lambda bi, si, j: (bi, si, 0, 0)),
            pl.BlockSpec((1, width, PROJ_TM), feat),
            pl.BlockSpec((1, IDX_HEADS, PROJ_TM), feat),
        ],
        out_shape=[
            jax.ShapeDtypeStruct((b, width, s), BF16),
            jax.ShapeDtypeStruct((b, s // ATT_KU, width, ATT_KU), BF16),
            jax.ShapeDtypeStruct((b, width, s), BF16),
            jax.ShapeDtypeStruct((b, IDX_HEADS, s), F32),
        ],
        compiler_params=pltpu.CompilerParams(
            dimension_semantics=("parallel", "parallel", "arbitrary"), vmem_limit_bytes=VMEM_LIMIT),
        name="proj_t",
    )(h3, wt_cat, wwt)


N_NEAR_UNITS = 2


def _bias_body(rb_ref, o_ref):
    h = pl.program_id(0)
    r = lax.broadcasted_iota(I32, (ATT_KU, ATT_TQ), 0)
    c = lax.broadcasted_iota(I32, (ATT_KU, ATT_TQ), 1)
    max_exact = NUM_BUCKETS // 2
    far = rb_ref[NUM_BUCKETS - 1, h]
    for t in range(N_NEAR_UNITS):
        dist = jnp.maximum(t * ATT_KU + c - r, 0)
        nf = jnp.maximum(dist, 1).astype(F32)
        large = max_exact + (jnp.log(nf / max_exact) / math.log(MAX_DISTANCE / max_exact)
                             * (NUM_BUCKETS - max_exact)).astype(I32)
        large = jnp.minimum(large, NUM_BUCKETS - 1)
        bucket = jnp.where(dist < max_exact, dist, large)
        val = jnp.zeros((ATT_KU, ATT_TQ), F32)
        for bkt in range(NUM_BUCKETS):
            val = jnp.where(bucket == bkt, rb_ref[bkt, h], val)
        o_ref[0, t] = (val - far) * LOG2E


def _bias_tiles(rel_bias):
    assert N_NEAR_UNITS * ATT_KU - (ATT_KU - 1) >= MAX_DISTANCE
    return pl.pallas_call(
        _bias_body,
        grid=(ATT_HEADS,),
        in_specs=[pl.BlockSpec(memory_space=pltpu.SMEM)],
        out_specs=pl.BlockSpec((1, N_NEAR_UNITS, ATT_KU, ATT_TQ), lambda h: (h, 0, 0, 0)),
        out_shape=jax.ShapeDtypeStruct((ATT_HEADS, N_NEAR_UNITS, ATT_KU, ATT_TQ), F32),
        name="bias_tiles",
    )(rel_bias)


def _tree_sum(xs):
    while len(xs) > 1:
        xs = [a + b for a, b in zip(xs[::2], xs[1::2])] + ([xs[-1]] if len(xs) % 2 else [])
    return xs[0]


def _sublane_allmax(x):
    for shift in (4, 2, 1):
        x = jnp.maximum(x, pltpu.roll(x, shift, axis=0))
    return x


def _attn_body(kidx_ref, qit_ref, wit_ref, k_ref, vt_ref, qt_ref, bias_ref, o_ref,
               key_scr, hi_scr, lo_scr, m_scr, acc_scr, *, top_k):
    i = pl.program_id(1)
    ku, tq, sl, pk = ATT_KU, ATT_TQ, SUBLANES, BF16_TILE_ROWS
    n_units = i + 1
    q0 = i * tq

    w_all = wit_ref[0] * (IDX_HEADS ** -0.5 * IDX_DIM ** -0.5)
    row = lax.broadcasted_iota(I32, (ku, tq), 0)
    col = lax.broadcasted_iota(I32, (ku, tq), 1)

    def score_unit(u, carry):
        r0 = pl.multiple_of(u * ku, ku)
        kch = kidx_ref[0, pl.ds(r0, ku), :]
        acc = jnp.zeros((ku, tq), F32)
        for h in range(IDX_HEADS):
            d = _dot(kch, qit_ref[0, h * IDX_DIM:(h + 1) * IDX_DIM, :])
            acc = acc + jnp.maximum(d, 0.0) * w_all[h:h + 1, :]
        bits = lax.bitcast_convert_type(acc, I32)
        key = jnp.where(bits < 0, bits ^ jnp.int32(0x7FFFFFFF), bits)
        key = jnp.where(r0 + row <= q0 + col, key, INT_MIN)
        key_scr[pl.ds(r0, ku), :] = key
        hi_scr[pl.ds(r0, ku), :] = lax.shift_right_arithmetic(key, 16).astype(I16)
        lo_scr[pl.ds(r0, ku), :] = ((key & 0xFFFF) - I16_BIAS).astype(I16)
        return carry

    lax.fori_loop(0, n_units, score_unit, 0)

    one_b, zero_b = jnp.ones((), BF16), jnp.zeros((), BF16)

    n_pairs = (n_units + 1) // 2

    @pl.when(n_units % 2 == 1)
    def _():
        r0 = pl.multiple_of(n_units * ku, ku)
        hi_scr[pl.ds(r0, ku), :] = jnp.full((ku, tq), I16_MIN, I16)
        lo_scr[pl.ds(r0, ku), :] = jnp.full((ku, tq), I16_MIN, I16)

    def count_ge(ref, cand):
        cand16 = jnp.broadcast_to(cand, (pk, tq)).astype(I16)

        def body(p, cnt):
            r0 = pl.multiple_of(p * (2 * ku), 2 * ku)
            blk = ref[pl.ds(r0, 2 * ku), :].reshape(2 * ku // pk, pk, tq)
            hit = jnp.where(blk >= cand16[None], one_b, zero_b)
            return cnt + _tree_sum([hit[g] for g in range(2 * ku // pk)]).astype(F32)

        cnt = lax.fori_loop(0, n_pairs, body, jnp.zeros((pk, tq), F32))
        return jnp.sum(cnt, axis=0, keepdims=True)

    def kth_largest(ref, kvec):
        zero = jnp.zeros((1, tq), I32)
        c0 = count_ge(ref, zero)
        ok0 = c0 >= kvec
        state = (jnp.where(ok0, zero, I16_MIN), jnp.where(ok0, c0, COUNT_ALL), jnp.where(ok0, 0.0, c0))

        def descend(t, state):
            prefix, n_ge, n_gt = state
            cand = prefix | jnp.left_shift(jnp.int32(1), 14 - t)
            c = count_ge(ref, cand)
            ok = c >= kvec
            return jnp.where(ok, cand, prefix), jnp.where(ok, c, n_ge), jnp.where(ok, n_gt, c)

        return lax.fori_loop(0, 15, descend, state)

    k_f = jnp.full((1, tq), top_k, F32)
    thr_hi, _, n_above = kth_largest(hi_scr, k_f)
    k_low = k_f - n_above
    thr_hi16 = jnp.broadcast_to(thr_hi, (pk, tq)).astype(I16)

    def keep_low_of_winners(p, carry):
        r0 = pl.multiple_of(p * (2 * ku), 2 * ku)
        hi = hi_scr[pl.ds(r0, 2 * ku), :].reshape(2 * ku // pk, pk, tq)
        lo = lo_scr[pl.ds(r0, 2 * ku), :].reshape(2 * ku // pk, pk, tq)
        lo_scr[pl.ds(r0, 2 * ku), :] = jnp.where(hi == thr_hi16[None], lo,
                                                 jnp.int16(I16_MIN)).reshape(2 * ku, tq)
        return carry

    lax.fori_loop(0, n_pairs, keep_low_of_winners, 0)
    thr_lo, n_ge_lo, n_greater = kth_largest(lo_scr, k_low)
    thr_raw = thr_hi * (1 << 16) + (thr_lo + I16_BIAS)
    thr = jnp.maximum(thr_raw, INT_MIN + 1)

    n_tied_ok = k_low - n_greater
    n_tied = n_ge_lo - n_greater
    has_excess = jnp.logical_and(n_tied > n_tied_ok, thr_raw > INT_MIN)

    @pl.when(jnp.max(jnp.where(has_excess, 1.0, 0.0)) > 0.0)
    def _():
        earlier = (lax.broadcasted_iota(I32, (ku, ku), 0) > lax.broadcasted_iota(I32, (ku, ku), 1))
        earlier = jnp.where(earlier, 1.0, 0.0).astype(BF16)

        def demote(u, seen):
            r0 = pl.multiple_of(u * ku, ku)
            key = key_scr[pl.ds(r0, ku), :]
            tied = key == thr
            rank = _dot(earlier, jnp.where(tied, 1.0, 0.0).astype(BF16)) + seen
            key_scr[pl.ds(r0, ku), :] = jnp.where(jnp.logical_and(tied, rank >= n_tied_ok), INT_MIN, key)
            return seen + jnp.sum(jnp.where(tied, 1.0, 0.0), axis=0, keepdims=True)

        lax.fori_loop(0, n_units, demote, jnp.zeros((1, tq), F32))

    acc_scr[...] = jnp.zeros(acc_scr.shape, F32)
    ones_rows = jnp.ones((ACC_ROWS - ATT_HEAD_DIM, ku), BF16)

    def unit(u, near_tile, online):
        r0 = pl.multiple_of(u * ku, ku)
        keep = key_scr[pl.ds(r0, ku), :].reshape(ku // sl, sl, tq) >= thr[None]
        mask_add = jnp.where(keep, 0.0, MASKED_LOGIT)

        def qk(h):
            hs = slice(h * ATT_HEAD_DIM, (h + 1) * ATT_HEAD_DIM)
            s = _dot(k_ref[0, pl.ds(r0, ku), hs], qt_ref[0, hs, :])
            if near_tile is not None:
                s = s + bias_ref[h, near_tile]
            return s

        ahead = QK_AHEAD
        pending = [qk(h) for h in range(min(ahead, ATT_HEADS))]
        for h in range(ATT_HEADS):
            hs = slice(h * ATT_HEAD_DIM, (h + 1) * ATT_HEAD_DIM)
            s = pending.pop(0)
            if h + ahead < ATT_HEADS:
                pending.append(qk(h + ahead))
            s = s.reshape(ku // sl, sl, tq) + mask_add
            if online:
                m_old = m_scr[h]
                m_new = jnp.maximum(m_old, _sublane_allmax(jnp.max(s, axis=0)))
                alpha = jnp.exp2(m_old - m_new)
                m_scr[h] = m_new
                p = jnp.exp2(s - m_new[None])
            else:
                p = jnp.exp2(s)
            v_ext = jnp.concatenate([vt_ref[0, u, hs, :], ones_rows], axis=0)
            pv = _dot(v_ext, p.reshape(ku, tq).astype(BF16))
            if online:
                acc = acc_scr[h].reshape(ACC_ROWS // sl, sl, tq) * alpha[None]
                acc_scr[h] = pv + acc.reshape(ACC_ROWS, tq)
            else:
                acc_scr[h] = pv + acc_scr[h]

    def run_units(online):
        lax.fori_loop(0, n_units - N_NEAR_UNITS, lambda u, c: (unit(u, None, online), c)[1], 0)
        for t in range(N_NEAR_UNITS - 1, -1, -1):
            lax.fori_loop(0, jnp.minimum(n_units - t, 1), lambda _, c, t=t: (unit(i - t, t, online), c)[1], 0)

    run_units(False)
    unsafe = jnp.zeros((1, tq), F32)
    for h in range(ATT_HEADS):
        l = acc_scr[h, ATT_HEAD_DIM:ATT_HEAD_DIM + 1, :]
        unsafe = jnp.maximum(unsafe, jnp.where(l < DENOM_MAX, jnp.where(l >= DENOM_MIN, 0.0, 1.0), 1.0))

    @pl.when(jnp.max(unsafe) > 0.0)
    def _():
        m_scr[...] = jnp.full(m_scr.shape, MASKED_LOGIT, F32)
        acc_scr[...] = jnp.zeros(acc_scr.shape, F32)
        run_units(True)

    for h in range(ATT_HEADS):
        hs = slice(h * ATT_HEAD_DIM, (h + 1) * ATT_HEAD_DIM)
        l = acc_scr[h, ATT_HEAD_DIM:ATT_HEAD_DIM + 1, :]
        o_ref[0, :, hs] = (acc_scr[h, :ATT_HEAD_DIM, :] / l).T.astype(BF16)


def _attention(kidx, qit, wit, k, vt, qt, bias, top_k):
    b, s, width = k.shape
    one = pl.Buffered(1)
    return pl.pallas_call(
        functools.partial(_attn_body, top_k=top_k),
        grid=(b, s // ATT_TQ),
        in_specs=[
            pl.BlockSpec((1, s, IDX_DIM), lambda bi, i: (bi, 0, 0), pipeline_mode=one),
            pl.BlockSpec((1, IDX_HEADS * IDX_DIM, ATT_TQ), lambda bi, i: (bi, 0, i)),
            pl.BlockSpec((1, IDX_HEADS, ATT_TQ), lambda bi, i: (bi, 0, i)),
            pl.BlockSpec((1, s, width), lambda bi, i: (bi, 0, 0), pipeline_mode=one),
            pl.BlockSpec((1, s // ATT_KU, width, ATT_KU), lambda bi, i: (bi, 0, 0, 0), pipeline_mode=one),
            pl.BlockSpec((1, width, ATT_TQ), lambda bi, i: (bi, 0, i)),
            pl.BlockSpec(bias.shape, lambda bi, i: (0, 0, 0, 0), pipeline_mode=one),
        ],
        out_specs=pl.BlockSpec((1, ATT_TQ, width), lambda bi, i: (bi, i, 0)),
        out_shape=jax.ShapeDtypeStruct((b, s, width), BF16),
        scratch_shapes=[
            pltpu.VMEM((s, ATT_TQ), I32),
            pltpu.VMEM((s, ATT_TQ), I16),
            pltpu.VMEM((s, ATT_TQ), I16),
            pltpu.VMEM((ATT_HEADS, SUBLANES, ATT_TQ), F32),
            pltpu.VMEM((ATT_HEADS, ACC_ROWS, ATT_TQ), F32),
        ],
        compiler_params=pltpu.CompilerParams(
            dimension_semantics=("parallel", "arbitrary"), vmem_limit_bytes=VMEM_LIMIT),
        name="dsa_attention",
    )(kidx, qit, wit, k, vt, qt, bias)


def _merge_body(x_ref, u_ref, vln_ref, yb_ref, gate_ref, ws_ref, bs_ref, wa_ref, wb_ref, wo_ref,
                gpost_ref, o_ref, ya_scr):
    ch = SGU_CHUNK
    d = x_ref.shape[1]
    tril = (lax.broadcasted_iota(I32, (ch, ch), 0) >= lax.broadcasted_iota(I32, (ch, ch), 1))
    for g in range(SGU_GROUPS):
        gs = slice(g * ch, (g + 1) * ch)
        wsg = jnp.where(tril, ws_ref[g], 0.0).astype(BF16)
        for c in range(MERGE_TM // ch):
            cs = slice(c * ch, (c + 1) * ch)
            mixed = _dot(wsg, vln_ref[cs, gs]) + bs_ref[g]
            ya_scr[cs, gs] = (u_ref[cs, gs].astype(F32) * mixed).astype(BF16)
    ma = _dot(ya_scr[...], wa_ref[...])
    mb = _dot(yb_ref[...], wb_ref[...])
    merged = gate_ref[:, :d].astype(F32) * ma + gate_ref[:, d:].astype(F32) * mb
    o = _dot(merged.astype(BF16), wo_ref[...])
    o_ref[...] = x_ref[...] + _rms(o, gpost_ref[...])


def _merge(x2, u, vln, yb, gate, w_s, b_s, w_a, w_b, w_o, g_post, layer):
    n, d = x2.shape
    width = u.shape[1]
    one = pl.Buffered(1)
    tok = lambda i: (i, 0)
    const2 = lambda i: (0, 0)
    whole = lambda a: pl.BlockSpec((None,) + a.shape[1:], lambda i: (layer,) + (0,) * (a.ndim - 1),
                                   pipeline_mode=one)
    return pl.pallas_call(
        _merge_body,
        grid=(n // MERGE_TM,),
        in_specs=[
            pl.BlockSpec((MERGE_TM, d), tok),
            pl.BlockSpec((MERGE_TM, width), tok),
            pl.BlockSpec((MERGE_TM, width), tok),
            pl.BlockSpec((MERGE_TM, width), tok),
            pl.BlockSpec((MERGE_TM, 2 * d), tok),
            whole(w_s), whole(b_s), whole(w_a), whole(w_b), whole(w_o),
            pl.BlockSpec((1, d), const2),
        ],
        out_specs=pl.BlockSpec((MERGE_TM, d), tok),
        out_shape=jax.ShapeDtypeStruct((n, d), F32),
        scratch_shapes=[pltpu.VMEM((MERGE_TM, width), BF16)],
        compiler_params=pltpu.CompilerParams(
            dimension_semantics=("parallel",), vmem_limit_bytes=VMEM_LIMIT),
        name="merge",
    )(x2, u, vln, yb, gate, w_s, b_s, w_a, w_b, w_o, g_post.reshape(1, d))


def kernel(x, ffn1_norm_pre, ffn1_norm_post, ffn1_w_in, ffn1_w_out, mix_norm_pre, mix_norm_post, w_in,
           sgu_ln_g, sgu_ln_b, sgu_w_s, sgu_b, rel_bias, w_branch_a, w_branch_b, w_gate, w_out,
           ffn2_norm_pre, ffn2_norm_post, ffn2_w_in, ffn2_w_out):
    b, s, d = x.shape
    depth = w_in.shape[0]
    sgu_w = sgu_ln_g.shape[1]
    att_w = ATT_HEADS * ATT_HEAD_DIM
    idx_w = IDX_HEADS * IDX_DIM
    top_k = min(TOPK_MAX, s // 4)
    assert s % PROJ_TM == 0 and s % ATT_TQ == 0 and (b * s) % FFN_TM == 0
    assert sgu_w == PROJ_BN and att_w == PROJ_BN and idx_w == PROJ_BN and ATT_KU == ATT_TQ

    sizes = (sgu_w, sgu_w, att_w, att_w, att_w, idx_w, IDX_DIM, IDX_HEADS)
    offs = [0]
    for sz in sizes:
        offs.append(offs[-1] + sz)
    col = lambda w, idx: w[:, :, offs[idx]:offs[idx + 1]]
    bf = lambda w: w.astype(BF16)
    w_gate_b = bf(w_gate)
    w_nat = jnp.concatenate([bf(col(w_in, 0)), bf(col(w_in, 1)), bf(col(w_in, 3))], axis=2)
    nat_blocks = (0, 1, 2)
    wt_cat = jnp.swapaxes(jnp.concatenate([bf(col(w_in, 2)), bf(col(w_in, 4)), bf(col(w_in, 5))], axis=2), 1, 2)
    wwt = jnp.swapaxes(bf(col(w_in, 7)), 1, 2)
    w_kidx = bf(col(w_in, 6))
    ffn1_in_b, ffn1_out_b, ffn2_in_b, ffn2_out_b = bf(ffn1_w_in), bf(ffn1_w_out), bf(ffn2_w_in), bf(ffn2_w_out)
    w_a_b, w_b_b, w_o_b = bf(w_branch_a), bf(w_branch_b), bf(w_out)
    b_s = sgu_b.reshape(depth, SGU_GROUPS, SGU_CHUNK, 1)

    bias = _bias_tiles(rel_bias)
    x2 = x.reshape(b * s, d)
    for l in range(depth):
        x2, h2 = _ffn(x2, ffn1_norm_pre[l], ffn1_norm_post[l], ffn1_in_b, ffn1_out_b, l,
                      g_next=mix_norm_pre[l])
        u, vln, k, gate, kidx = _proj_nat(h2, w_nat, w_kidx, w_gate_b, sgu_ln_g[l], sgu_ln_b[l], l,
                                          nat_blocks)
        qt, vt, qit, wit = _proj_t(h2.reshape(b, s, d), wt_cat, wwt, l)
        yb = _attention(kidx.reshape(b, s, IDX_DIM), qit, wit, k.reshape(b, s, att_w), vt, qt, bias, top_k)
        x2 = _merge(x2, u, vln, yb.reshape(b * s, att_w), gate,
                    sgu_w_s, b_s, w_a_b, w_b_b, w_o_b, mix_norm_post[l], l)
        (x2,) = _ffn(x2, ffn2_norm_pre[l], ffn2_norm_post[l], ffn2_in_b, ffn2_out_b, l)
    return x2.reshape(b, s, d)
```

```python
import functools
import math

import jax
import jax.numpy as jnp
from jax import lax
from jax.experimental import pallas as pl
from jax.experimental.pallas import tpu as pltpu

F32 = jnp.float32
BF16 = jnp.bfloat16
I32 = jnp.int32
I16 = jnp.int16

SGU_GROUPS = 8
SGU_CHUNK = 128
ATT_HEADS = 8
ATT_HEAD_DIM = 128
IDX_HEADS = 16
IDX_DIM = 64
TOPK_MAX = 256
NUM_BUCKETS = 32
MAX_DISTANCE = 128
NORM_EPS = 1e-6
LN_EPS = 1e-5

V7X_VMEM_BYTES = 64 * 1024 * 1024
VMEM_LIMIT = V7X_VMEM_BYTES - 8 * 1024 * 1024
SUBLANES = 8
BF16_TILE_ROWS = 16

FFN_TM = 512
FFN_TF = 512
FFN_HALF = 256
PROJ_TM = 1024
PROJ_BN = 1024
PROJ_CH = 256
MM_AHEAD = 4
MERGE_TM = 256
ATT_TQ = 256
ATT_KU = 256
ACC_ROWS = ATT_HEAD_DIM + BF16_TILE_ROWS
QK_AHEAD = ATT_HEADS
DENOM_MIN, DENOM_MAX = 2.0 ** -60, 2.0 ** 20

INT_MIN = -(2 ** 31)
I16_MIN, I16_BIAS = -(2 ** 15), 2 ** 15
COUNT_ALL = 2.0 ** 30
MASKED_LOGIT = -1e30
LOG2E = math.log2(math.e)


def _rms(xf, g, scale=1.0):
    ms = jnp.mean(xf * xf, axis=-1, keepdims=True)
    return xf * (scale * lax.rsqrt(ms + NORM_EPS)) * g


def _gelu_tanh(x):
    c = math.sqrt(2.0 / math.pi)
    return x * (0.5 * (1.0 + jnp.tanh(c * (x + 0.044715 * (x * x * x)))))


def _dot(a, b):
    return jnp.dot(a, b, preferred_element_type=F32)


def _dot_nt(a, b):
    return lax.dot_general(a, b, (((1,), (1,)), ((), ())), preferred_element_type=F32)


def _ffn_body(*refs, emit_next):
    if emit_next:
        x_ref, gpre_ref, gpost_ref, gnext_ref, wa_ref, wb_ref, wo_ref, o_ref, hn_ref, h_scr, acc_scr = refs
    else:
        x_ref, gpre_ref, gpost_ref, wa_ref, wb_ref, wo_ref, o_ref, h_scr, acc_scr = refs
    j = pl.program_id(1)

    @pl.when(j == 0)
    def _():
        h_scr[...] = _rms(x_ref[...], gpre_ref[...]).astype(BF16)
        acc_scr[...] = jnp.zeros_like(acc_scr)

    h = h_scr[...]
    gs = []
    for c in range(FFN_TF // FFN_HALF):
        cols = slice(c * FFN_HALF, (c + 1) * FFN_HALF)
        a = _dot(h, wa_ref[:, cols])
        b = _dot(h, wb_ref[:, cols])
        gs.append((a * jax.nn.sigmoid(a) * b).astype(BF16))
    acc_scr[...] += _dot(jnp.concatenate(gs, axis=1), wo_ref[...])

    @pl.when(j == pl.num_programs(1) - 1)
    def _():
        y = x_ref[...] + _rms(acc_scr[...], gpost_ref[...], scale=0.5)
        o_ref[...] = y
        if emit_next:
            hn_ref[...] = _rms(y, gnext_ref[...]).astype(BF16)


def _ffn(x2, g_pre, g_post, w_in, w_out, layer, g_next=None):
    n, d = x2.shape
    d_ff = w_out.shape[1]
    nf = d_ff // FFN_TF
    emit_next = g_next is not None
    vec = pl.BlockSpec((1, d), lambda i, j: (0, 0))
    tok = pl.BlockSpec((FFN_TM, d), lambda i, j: (i, 0))
    gains = [g_pre, g_post] + ([g_next] if emit_next else [])
    out_shape = [jax.ShapeDtypeStruct((n, d), F32)] + ([jax.ShapeDtypeStruct((n, d), BF16)] if emit_next else [])
    return pl.pallas_call(
        functools.partial(_ffn_body, emit_next=emit_next),
        grid=(n // FFN_TM, nf),
        in_specs=[tok] + [vec] * len(gains) + [
            pl.BlockSpec((None, d, FFN_TF), lambda i, j: (layer, 0, j)),
            pl.BlockSpec((None, d, FFN_TF), lambda i, j: (layer, 0, j + nf)),
            pl.BlockSpec((None, FFN_TF, d), lambda i, j: (layer, j, 0)),
        ],
        out_specs=[tok] * len(out_shape),
        out_shape=out_shape,
        scratch_shapes=[pltpu.VMEM((FFN_TM, d), BF16), pltpu.VMEM((FFN_TM, d), F32)],
        compiler_params=pltpu.CompilerParams(
            dimension_semantics=("parallel", "arbitrary"), vmem_limit_bytes=VMEM_LIMIT),
        name="ffn",
    )(x2, *[g.reshape(1, d) for g in gains], w_in, w_in, w_out)


def _chunked(n_chunks, matmul, epilogue):
    z = [matmul(c) for c in range(min(MM_AHEAD, n_chunks))]
    for c in range(n_chunks):
        if c + MM_AHEAD < n_chunks:
            z.append(matmul(c + MM_AHEAD))
        epilogue(c, z[c])
        z[c] = None


def _proj_nat_body(h_ref, w_ref, wki_ref, wg_ref, lng_ref, lnb_ref,
                   u_ref, vln_ref, k_ref, gate_ref, kidx_ref, v_scr):
    j = pl.program_id(1)
    ch = PROJ_CH
    nc = PROJ_BN // ch
    cols = lambda c: slice(c * ch, (c + 1) * ch)
    matmul = lambda c: _dot(h_ref[...], w_ref[:, cols(c)])
    matmul_gate = lambda c: _dot(h_ref[...], wg_ref[:, cols(c)])

    @pl.when(j == 0)
    def _():
        kidx_ref[...] = _dot(h_ref[...], wki_ref[...]).astype(BF16)

        def store_u(c, z):
            u_ref[:, cols(c)] = _gelu_tanh(z).astype(BF16)
        _chunked(nc, matmul, store_u)

    @pl.when(j == 1)
    def _():
        def store_v(c, z):
            v_scr[:, cols(c)] = _gelu_tanh(z)
        _chunked(nc, matmul, store_v)
        v = v_scr[...]
        mu = jnp.mean(v, axis=-1, keepdims=True)
        vc = v - mu
        var = jnp.mean(vc * vc, axis=-1, keepdims=True)
        vln_ref[...] = (vc * lax.rsqrt(var + LN_EPS) * lng_ref[...] + lnb_ref[...]).astype(BF16)

    @pl.when(j == 2)
    def _():
        def store_k(c, z):
            k_ref[:, cols(c)] = z.astype(BF16)
        _chunked(nc, matmul, store_k)

    @pl.when(j >= 3)
    def _():
        def store_gate(c, z):
            gate_ref[:, cols(c)] = jax.nn.sigmoid(z).astype(BF16)
        _chunked(nc, matmul_gate, store_gate)


def _proj_nat(h2, w_in, w_kidx, w_gate, ln_g, ln_b, layer, blocks):
    n, d = h2.shape
    width = PROJ_BN
    ngate = w_gate.shape[2]
    n_in = len(blocks)
    assert n_in == 3
    nj = n_in + ngate // width
    tok = lambda i, j: (i, 0)

    def in_block(i, j):
        blk = blocks[n_in - 1]
        for t in range(n_in - 2, -1, -1):
            blk = jnp.where(j == t, blocks[t], blk)
        return (layer, 0, blk)

    return pl.pallas_call(
        _proj_nat_body,
        grid=(n // PROJ_TM, nj),
        in_specs=[
            pl.BlockSpec((PROJ_TM, d), tok),
            pl.BlockSpec((None, d, width), in_block),
            pl.BlockSpec((None, d, IDX_DIM), lambda i, j: (layer, 0, 0)),
            pl.BlockSpec((None, d, width), lambda i, j: (layer, 0, jnp.maximum(j - n_in, 0))),
            pl.BlockSpec((1, width), lambda i, j: (0, 0)),
            pl.BlockSpec((1, width), lambda i, j: (0, 0)),
        ],
        out_specs=[
            pl.BlockSpec((PROJ_TM, width), tok),
            pl.BlockSpec((PROJ_TM, width), tok),
            pl.BlockSpec((PROJ_TM, width), tok),
            pl.BlockSpec((PROJ_TM, width), lambda i, j: (i, jnp.maximum(j - 3, 0))),
            pl.BlockSpec((PROJ_TM, IDX_DIM), tok),
        ],
        out_shape=[
            jax.ShapeDtypeStruct((n, width), BF16),
            jax.ShapeDtypeStruct((n, width), BF16),
            jax.ShapeDtypeStruct((n, width), BF16),
            jax.ShapeDtypeStruct((n, ngate), BF16),
            jax.ShapeDtypeStruct((n, IDX_DIM), BF16),
        ],
        scratch_shapes=[pltpu.VMEM((PROJ_TM, width), F32)],
        compiler_params=pltpu.CompilerParams(
            dimension_semantics=("parallel", "arbitrary"), vmem_limit_bytes=VMEM_LIMIT),
        name="proj_nat",
    )(h2, w_in, w_kidx, w_gate, ln_g.reshape(1, width), ln_b.reshape(1, width))


def _proj_t_body(h_ref, wt_ref, wwt_ref, qt_ref, vt_ref, qit_ref, wit_ref):
    j = pl.program_id(2)
    ch = PROJ_CH
    nc = PROJ_BN // ch
    rows = lambda c: slice(c * ch, (c + 1) * ch)
    matmul = lambda c: _dot_nt(wt_ref[rows(c), :], h_ref[0])

    @pl.when(j == 0)
    def _():
        wit_ref[0] = _dot_nt(wwt_ref[...], h_ref[0])

        def store_q(c, zt):
            qt_ref[0, rows(c), :] = (zt * (ATT_HEAD_DIM ** -0.5 * LOG2E)).astype(BF16)
        _chunked(nc, matmul, store_q)

    @pl.when(j == 1)
    def _():
        def store_v(c, zt):
            for cc in range(PROJ_TM // ATT_KU):
                vt_ref[0, cc, rows(c), :] = zt[:, cc * ATT_KU:(cc + 1) * ATT_KU].astype(BF16)
        _chunked(nc, matmul, store_v)

    @pl.when(j == 2)
    def _():
        def store_qi(c, zt):
            qit_ref[0, rows(c), :] = zt.astype(BF16)
        _chunked(nc, matmul, store_qi)


def _proj_t(h3, wt_cat, wwt, layer):
    b, s, d = h3.shape
    width = PROJ_BN
    nch = PROJ_TM // ATT_KU
    feat = lambda bi, si, j: (bi, 0, si)
    return pl.pallas_call(
        _proj_t_body,
        grid=(b, s // PROJ_TM, 3),
        in_specs=[
            pl.BlockSpec((1, PROJ_TM, d), lambda bi, si, j: (bi, si, 0)),
            pl.BlockSpec((None, width, d), lambda bi, si, j: (layer, j, 0)),
            pl.BlockSpec((None, IDX_HEADS, d), lambda bi, si, j: (layer, 0, 0)),
        ],
        out_specs=[
            pl.BlockSpec((1, width, PROJ_TM), feat),
            pl.BlockSpec((1, nch, width, ATT_KU), lambda bi, si, j: (bi, si, 0, 0)),
            pl.BlockSpec((1, width, PROJ_TM), feat),
            pl.BlockSpec((1, IDX_HEADS, PROJ_TM), feat),
        ],
        out_shape=[
            jax.ShapeDtypeStruct((b, width, s), BF16),
            jax.ShapeDtypeStruct((b, s // ATT_KU, width, ATT_KU), BF16),
            jax.ShapeDtypeStruct((b, width, s), BF16),
            jax.ShapeDtypeStruct((b, IDX_HEADS, s), F32),
        ],
        compiler_params=pltpu.CompilerParams(
            dimension_semantics=("parallel", "parallel", "arbitrary"), vmem_limit_bytes=VMEM_LIMIT),
        name="proj_t",
    )(h3, wt_cat, wwt)


N_NEAR_UNITS = 2


def _bias_body(rb_ref, o_ref):
    h = pl.program_id(0)
    r = lax.broadcasted_iota(I32, (ATT_KU, ATT_TQ), 0)
    c = lax.broadcasted_iota(I32, (ATT_KU, ATT_TQ), 1)
    max_exact = NUM_BUCKETS // 2
    far = rb_ref[NUM_BUCKETS - 1, h]
    for t in range(N_NEAR_UNITS):
        dist = jnp.maximum(t * ATT_KU + c - r, 0)
        nf = jnp.maximum(dist, 1).astype(F32)
        large = max_exact + (jnp.log(nf / max_exact) / math.log(MAX_DISTANCE / max_exact)
                             * (NUM_BUCKETS - max_exact)).astype(I32)
        large = jnp.minimum(large, NUM_BUCKETS - 1)
        bucket = jnp.where(dist < max_exact, dist, large)
        val = jnp.zeros((ATT_KU, ATT_TQ), F32)
        for bkt in range(NUM_BUCKETS):
            val = jnp.where(bucket == bkt, rb_ref[bkt, h], val)
        o_ref[0, t] = (val - far) * LOG2E


def _bias_tiles(rel_bias):
    assert N_NEAR_UNITS * ATT_KU - (ATT_KU - 1) >= MAX_DISTANCE
    return pl.pallas_call(
        _bias_body,
        grid=(ATT_HEADS,),
        in_specs=[pl.BlockSpec(memory_space=pltpu.SMEM)],
        out_specs=pl.BlockSpec((1, N_NEAR_UNITS, ATT_KU, ATT_TQ), lambda h: (h, 0, 0, 0)),
        out_shape=jax.ShapeDtypeStruct((ATT_HEADS, N_NEAR_UNITS, ATT_KU, ATT_TQ), F32),
        name="bias_tiles",
    )(rel_bias)


def _tree_sum(xs):
    while len(xs) > 1:
        xs = [a + b for a, b in zip(xs[::2], xs[1::2])] + ([xs[-1]] if len(xs) % 2 else [])
    return xs[0]


def _sublane_allmax(x):
    for shift in (4, 2, 1):
        x = jnp.maximum(x, pltpu.roll(x, shift, axis=0))
    return x


def _attn_body(kidx_ref, qit_ref, wit_ref, k_ref, vt_ref, qt_ref, bias_ref, o_ref,
               key_scr, hi_scr, lo_scr, m_scr, acc_scr, *, top_k):
    i = pl.program_id(1)
    ku, tq, sl, pk = ATT_KU, ATT_TQ, SUBLANES, BF16_TILE_ROWS
    n_units = i + 1
    q0 = i * tq

    w_all = wit_ref[0] * (IDX_HEADS ** -0.5 * IDX_DIM ** -0.5)
    row = lax.broadcasted_iota(I32, (ku, tq), 0)
    col = lax.broadcasted_iota(I32, (ku, tq), 1)

    def score_unit(u, carry):
        r0 = pl.multiple_of(u * ku, ku)
        kch = kidx_ref[0, pl.ds(r0, ku), :]
        acc = jnp.zeros((ku, tq), F32)
        for h in range(IDX_HEADS):
            d = _dot(kch, qit_ref[0, h * IDX_DIM:(h + 1) * IDX_DIM, :])
            acc = acc + jnp.maximum(d, 0.0) * w_all[h:h + 1, :]
        bits = lax.bitcast_convert_type(acc, I32)
        key = jnp.where(bits < 0, bits ^ jnp.int32(0x7FFFFFFF), bits)
        key = jnp.where(r0 + row <= q0 + col, key, INT_MIN)
        key_scr[pl.ds(r0, ku), :] = key
        hi_scr[pl.ds(r0, ku), :] = lax.shift_right_arithmetic(key, 16).astype(I16)
        lo_scr[pl.ds(r0, ku), :] = ((key & 0xFFFF) - I16_BIAS).astype(I16)
        return carry

    lax.fori_loop(0, n_units, score_unit, 0)

    one_b, zero_b = jnp.ones((), BF16), jnp.zeros((), BF16)

    n_pairs = (n_units + 1) // 2

    @pl.when(n_units % 2 == 1)
    def _():
        r0 = pl.multiple_of(n_units * ku, ku)
        hi_scr[pl.ds(r0, ku), :] = jnp.full((ku, tq), I16_MIN, I16)
        lo_scr[pl.ds(r0, ku), :] = jnp.full((ku, tq), I16_MIN, I16)

    def count_ge(ref, cand):
        cand16 = jnp.broadcast_to(cand, (pk, tq)).astype(I16)

        def body(p, cnt):
            r0 = pl.multiple_of(p * (2 * ku), 2 * ku)
            blk = ref[pl.ds(r0, 2 * ku), :].reshape(2 * ku // pk, pk, tq)
            hit = jnp.where(blk >= cand16[None], one_b, zero_b)
            return cnt + _tree_sum([hit[g] for g in range(2 * ku // pk)]).astype(F32)

        cnt = lax.fori_loop(0, n_pairs, body, jnp.zeros((pk, tq), F32))
        return jnp.sum(cnt, axis=0, keepdims=True)

    def kth_largest(ref, kvec):
        zero = jnp.zeros((1, tq), I32)
        c0 = count_ge(ref, zero)
        ok0 = c0 >= kvec
        state = (jnp.where(ok0, zero, I16_MIN), jnp.where(ok0, c0, COUNT_ALL), jnp.where(ok0, 0.0, c0))

        def descend(t, state):
            prefix, n_ge, n_gt = state
            cand = prefix | jnp.left_shift(jnp.int32(1), 14 - t)
            c = count_ge(ref, cand)
            ok = c >= kvec
            return jnp.where(ok, cand, prefix), jnp.where(ok, c, n_ge), jnp.where(ok, n_gt, c)

        return lax.fori_loop(0, 15, descend, state)

    k_f = jnp.full((1, tq), top_k, F32)
    thr_hi, _, n_above = kth_largest(hi_scr, k_f)
    k_low = k_f - n_above
    thr_hi16 = jnp.broadcast_to(thr_hi, (pk, tq)).astype(I16)

    def keep_low_of_winners(p, carry):
        r0 = pl.multiple_of(p * (2 * ku), 2 * ku)
        hi = hi_scr[pl.ds(r0, 2 * ku), :].reshape(2 * ku // pk, pk, tq)
        lo = lo_scr[pl.ds(r0, 2 * ku), :].reshape(2 * ku // pk, pk, tq)
        lo_scr[pl.ds(r0, 2 * ku), :] = jnp.where(hi == thr_hi16[None], lo,
                                                 jnp.int16(I16_MIN)).reshape(2 * ku, tq)
        return carry

    lax.fori_loop(0, n_pairs, keep_low_of_winners, 0)
    thr_lo, n_ge_lo, n_greater = kth_largest(lo_scr, k_low)
    thr_raw = thr_hi * (1 << 16) + (thr_lo + I16_BIAS)
    thr = jnp.maximum(thr_raw, INT_MIN + 1)

    n_tied_ok = k_low - n_greater
    n_tied = n_ge_lo - n_greater
    has_excess = jnp.logical_and(n_tied > n_tied_ok, thr_raw > INT_MIN)

    @pl.when(jnp.max(jnp.where(has_excess, 1.0, 0.0)) > 0.0)
    def _():
        earlier = (lax.broadcasted_iota(I32, (ku, ku), 0) > lax.broadcasted_iota(I32, (ku, ku), 1))
        earlier = jnp.where(earlier, 1.0, 0.0).astype(BF16)

        def demote(u, seen):
            r0 = pl.multiple_of(u * ku, ku)
            key = key_scr[pl.ds(r0, ku), :]
            tied = key == thr
            rank = _dot(earlier, jnp.where(tied, 1.0, 0.0).astype(BF16)) + seen
            key_scr[pl.ds(r0, ku), :] = jnp.where(jnp.logical_and(tied, rank >= n_tied_ok), INT_MIN, key)
            return seen + jnp.sum(jnp.where(tied, 1.0, 0.0), axis=0, keepdims=True)

        lax.fori_loop(0, n_units, demote, jnp.zeros((1, tq), F32))

    acc_scr[...] = jnp.zeros(acc_scr.shape, F32)
    ones_rows = jnp.ones((ACC_ROWS - ATT_HEAD_DIM, ku), BF16)

    def unit(u, near_tile, online):
        r0 = pl.multiple_of(u * ku, ku)
        keep = key_scr[pl.ds(r0, ku), :].reshape(ku // sl, sl, tq) >= thr[None]
        mask_add = jnp.where(keep, 0.0, MASKED_LOGIT)

        def qk(h):
            hs = slice(h * ATT_HEAD_DIM, (h + 1) * ATT_HEAD_DIM)
            s = _dot(k_ref[0, pl.ds(r0, ku), hs], qt_ref[0, hs, :])
            if near_tile is not None:
                s = s + bias_ref[h, near_tile]
            return s

        ahead = QK_AHEAD
        pending = [qk(h) for h in range(min(ahead, ATT_HEADS))]
        for h in range(ATT_HEADS):
            hs = slice(h * ATT_HEAD_DIM, (h + 1) * ATT_HEAD_DIM)
            s = pending.pop(0)
            if h + ahead < ATT_HEADS:
                pending.append(qk(h + ahead))
            s = s.reshape(ku // sl, sl, tq) + mask_add
            if online:
                m_old = m_scr[h]
                m_new = jnp.maximum(m_old, _sublane_allmax(jnp.max(s, axis=0)))
                alpha = jnp.exp2(m_old - m_new)
                m_scr[h] = m_new
                p = jnp.exp2(s - m_new[None])
            else:
                p = jnp.exp2(s)
            v_ext = jnp.concatenate([vt_ref[0, u, hs, :], ones_rows], axis=0)
            pv = _dot(v_ext, p.reshape(ku, tq).astype(BF16))
            if online:
                acc = acc_scr[h].reshape(ACC_ROWS // sl, sl, tq) * alpha[None]
                acc_scr[h] = pv + acc.reshape(ACC_ROWS, tq)
            else:
                acc_scr[h] = pv + acc_scr[h]

    def run_units(online):
        lax.fori_loop(0, n_units - N_NEAR_UNITS, lambda u, c: (unit(u, None, online), c)[1], 0)
        for t in range(N_NEAR_UNITS - 1, -1, -1):
            lax.fori_loop(0, jnp.minimum(n_units - t, 1), lambda _, c, t=t: (unit(i - t, t, online), c)[1], 0)

    run_units(False)
    unsafe = jnp.zeros((1, tq), F32)
    for h in range(ATT_HEADS):
        l = acc_scr[h, ATT_HEAD_DIM:ATT_HEAD_DIM + 1, :]
        unsafe = jnp.maximum(unsafe, jnp.where(l < DENOM_MAX, jnp.where(l >= DENOM_MIN, 0.0, 1.0), 1.0))

    @pl.when(jnp.max(unsafe) > 0.0)
    def _():
        m_scr[...] = jnp.full(m_scr.shape, MASKED_LOGIT, F32)
        acc_scr[...] = jnp.zeros(acc_scr.shape, F32)
        run_units(True)

    for h in range(ATT_HEADS):
        hs = slice(h * ATT_HEAD_DIM, (h + 1) * ATT_HEAD_DIM)
        l = acc_scr[h, ATT_HEAD_DIM:ATT_HEAD_DIM + 1, :]
        o_ref[0, :, hs] = (acc_scr[h, :ATT_HEAD_DIM, :] / l).T.astype(BF16)


def _attention(kidx, qit, wit, k, vt, qt, bias, top_k):
    b, s, width = k.shape
    one = pl.Buffered(1)
    return pl.pallas_call(
        functools.partial(_attn_body, top_k=top_k),
        grid=(b, s // ATT_TQ),
        in_specs=[
            pl.BlockSpec((1, s, IDX_DIM), lambda bi, i: (bi, 0, 0), pipeline_mode=one),
            pl.BlockSpec((1, IDX_HEADS * IDX_DIM, ATT_TQ), lambda bi, i: (bi, 0, i)),
            pl.BlockSpec((1, IDX_HEADS, ATT_TQ), lambda bi, i: (bi, 0, i)),
            pl.BlockSpec((1, s, width), lambda bi, i: (bi, 0, 0), pipeline_mode=one),
            pl.BlockSpec((1, s // ATT_KU, width, ATT_KU), lambda bi, i: (bi, 0, 0, 0), pipeline_mode=one),
            pl.BlockSpec((1, width, ATT_TQ), lambda bi, i: (bi, 0, i)),
            pl.BlockSpec(bias.shape, lambda bi, i: (0, 0, 0, 0), pipeline_mode=one),
        ],
        out_specs=pl.BlockSpec((1, ATT_TQ, width), lambda bi, i: (bi, i, 0)),
        out_shape=jax.ShapeDtypeStruct((b, s, width), BF16),
        scratch_shapes=[
            pltpu.VMEM((s, ATT_TQ), I32),
            pltpu.VMEM((s, ATT_TQ), I16),
            pltpu.VMEM((s, ATT_TQ), I16),
            pltpu.VMEM((ATT_HEADS, SUBLANES, ATT_TQ), F32),
            pltpu.VMEM((ATT_HEADS, ACC_ROWS, ATT_TQ), F32),
        ],
        compiler_params=pltpu.CompilerParams(
            dimension_semantics=("parallel", "arbitrary"), vmem_limit_bytes=VMEM_LIMIT),
        name="dsa_attention",
    )(kidx, qit, wit, k, vt, qt, bias)


def _merge_body(x_ref, u_ref, vln_ref, yb_ref, gate_ref, ws_ref, bs_ref, wa_ref, wb_ref, wo_ref,
                gpost_ref, o_ref, ya_scr):
    ch = SGU_CHUNK
    d = x_ref.shape[1]
    tril = (lax.broadcasted_iota(I32, (ch, ch), 0) >= lax.broadcasted_iota(I32, (ch, ch), 1))
    for g in range(SGU_GROUPS):
        gs = slice(g * ch, (g + 1) * ch)
        wsg = jnp.where(tril, ws_ref[g], 0.0).astype(BF16)
        for c in range(MERGE_TM // ch):
            cs = slice(c * ch, (c + 1) * ch)
            mixed = _dot(wsg, vln_ref[cs, gs]) + bs_ref[g]
            ya_scr[cs, gs] = (u_ref[cs, gs].astype(F32) * mixed).astype(BF16)
    ma = _dot(ya_scr[...], wa_ref[...])
    mb = _dot(yb_ref[...], wb_ref[...])
    merged = gate_ref[:, :d].astype(F32) * ma + gate_ref[:, d:].astype(F32) * mb
    o = _dot(merged.astype(BF16), wo_ref[...])
    o_ref[...] = x_ref[...] + _rms(o, gpost_ref[...])


def _merge(x2, u, vln, yb, gate, w_s, b_s, w_a, w_b, w_o, g_post, layer):
    n, d = x2.shape
    width = u.shape[1]
    one = pl.Buffered(1)
    tok = lambda i: (i, 0)
    const2 = lambda i: (0, 0)
    whole = lambda a: pl.BlockSpec((None,) + a.shape[1:], lambda i: (layer,) + (0,) * (a.ndim - 1),
                                   pipeline_mode=one)
    return pl.pallas_call(
        _merge_body,
        grid=(n // MERGE_TM,),
        in_specs=[
            pl.BlockSpec((MERGE_TM, d), tok),
            pl.BlockSpec((MERGE_TM, width), tok),
            pl.BlockSpec((MERGE_TM, width), tok),
            pl.BlockSpec((MERGE_TM, width), tok),
            pl.BlockSpec((MERGE_TM, 2 * d), tok),
            whole(w_s), whole(b_s), whole(w_a), whole(w_b), whole(w_o),
            pl.BlockSpec((1, d), const2),
        ],
        out_specs=pl.BlockSpec((MERGE_TM, d), tok),
        out_shape=jax.ShapeDtypeStruct((n, d), F32),
        scratch_shapes=[pltpu.VMEM((MERGE_TM, width), BF16)],
        compiler_params=pltpu.CompilerParams(
            dimension_semantics=("parallel",), vmem_limit_bytes=VMEM_LIMIT),
        name="merge",
    )(x2, u, vln, yb, gate, w_s, b_s, w_a, w_b, w_o, g_post.reshape(1, d))


def kernel(x, ffn1_norm_pre, ffn1_norm_post, ffn1_w_in, ffn1_w_out, mix_norm_pre, mix_norm_post, w_in,
           sgu_ln_g, sgu_ln_b, sgu_w_s, sgu_b, rel_bias, w_branch_a, w_branch_b, w_gate, w_out,
           ffn2_norm_pre, ffn2_norm_post, ffn2_w_in, ffn2_w_out):
    b, s, d = x.shape
    depth = w_in.shape[0]
    sgu_w = sgu_ln_g.shape[1]
    att_w = ATT_HEADS * ATT_HEAD_DIM
    idx_w = IDX_HEADS * IDX_DIM
    top_k = min(TOPK_MAX, s // 4)
    assert s % PROJ_TM == 0 and s % ATT_TQ == 0 and (b * s) % FFN_TM == 0
    assert sgu_w == PROJ_BN and att_w == PROJ_BN and idx_w == PROJ_BN and ATT_KU == ATT_TQ

    sizes = (sgu_w, sgu_w, att_w, att_w, att_w, idx_w, IDX_DIM, IDX_HEADS)
    offs = [0]
    for sz in sizes:
        offs.append(offs[-1] + sz)
    col = lambda w, idx: w[:, :, offs[idx]:offs[idx + 1]]
    bf = lambda w: w.astype(BF16)
    w_gate_b = bf(w_gate)
    w_nat = jnp.concatenate([bf(col(w_in, 0)), bf(col(w_in, 1)), bf(col(w_in, 3))], axis=2)
    nat_blocks = (0, 1, 2)
    wt_cat = jnp.swapaxes(jnp.concatenate([bf(col(w_in, 2)), bf(col(w_in, 4)), bf(col(w_in, 5))], axis=2), 1, 2)
    wwt = jnp.swapaxes(bf(col(w_in, 7)), 1, 2)
    w_kidx = bf(col(w_in, 6))
    ffn1_in_b, ffn1_out_b, ffn2_in_b, ffn2_out_b = bf(ffn1_w_in), bf(ffn1_w_out), bf(ffn2_w_in), bf(ffn2_w_out)
    w_a_b, w_b_b, w_o_b = bf(w_branch_a), bf(w_branch_b), bf(w_out)
    b_s = sgu_b.reshape(depth, SGU_GROUPS, SGU_CHUNK, 1)

    bias = _bias_tiles(rel_bias)
    x2 = x.reshape(b * s, d)
    for l in range(depth):
        x2, h2 = _ffn(x2, ffn1_norm_pre[l], ffn1_norm_post[l], ffn1_in_b, ffn1_out_b, l,
                      g_next=mix_norm_pre[l])
        u, vln, k, gate, kidx = _proj_nat(h2, w_nat, w_kidx, w_gate_b, sgu_ln_g[l], sgu_ln_b[l], l,
                                          nat_blocks)
        qt, vt, qit, wit = _proj_t(h2.reshape(b, s, d), wt_cat, wwt, l)
        yb = _attention(kidx.reshape(b, s, IDX_DIM), qit, wit, k.reshape(b, s, att_w), vt, qt, bias, top_k)
        x2 = _merge(x2, u, vln, yb.reshape(b * s, att_w), gate,
                    sgu_w_s, b_s, w_a_b, w_b_b, w_o_b, mix_norm_post[l], l)
        (x2,) = _ffn(x2, ffn2_norm_pre[l], ffn2_norm_post[l], ffn2_in_b, ffn2_out_b, l)
    return x2.reshape(b, s, d)
```
